```python
import math
import jax, jax.numpy as jnp
from jax import lax
import numpy as np

D_MODEL = 1024
BATCH = 8
SEQ = 2048
DEPTH = 2
DEC_BATCH = 128
DEC_SEQ = 4
PAST_LEN = 16384
PAGE_SIZE = 128

LRU_WIDTH = D_MODEL // 2
LRU_HEADS = 8
LRU_HEAD_DIM = LRU_WIDTH // LRU_HEADS
LRU_CONV = 4
LRU_C = 8.0
SC_WIDTH = D_MODEL // 2
SC_CONV = 3
SSM_WIDTH = D_MODEL // 2
SSM_GROUP = 16
SSM_GROUPS = SSM_WIDTH // SSM_GROUP
SSM_STATE = 64
N_BRANCH = 3
D_FF = 4 * D_MODEL
N_IN = 2 * LRU_WIDTH + 3 * SC_WIDTH + SSM_WIDTH + N_BRANCH * D_MODEL
SPLITS = [LRU_WIDTH, 2 * LRU_WIDTH, 2 * LRU_WIDTH + SC_WIDTH, 2 * LRU_WIDTH + 2 * SC_WIDTH,
          2 * LRU_WIDTH + 3 * SC_WIDTH, 2 * LRU_WIDTH + 3 * SC_WIDTH + SSM_WIDTH]
ALPHA = (2 * DEPTH) ** 0.25
BETA = (8 * DEPTH) ** -0.25
LN_EPS = 1e-5

kernel_name = "hybrid_rglru_shortconv_s5_step"

F32 = jnp.float32


def layer_norm(x, g, b):
    xf = x.astype(F32)
    mu = jnp.mean(xf, axis=-1, keepdims=True)
    var = jnp.mean(jnp.square(xf - mu), axis=-1, keepdims=True)
    y = (xf - mu) * lax.rsqrt(var + LN_EPS) * g.astype(F32) + b.astype(F32)
    return y.astype(x.dtype)


def causal_conv(u, buf, w):
    K = w.shape[0]
    L = u.shape[1]
    up = jnp.concatenate([buf.astype(u.dtype), u], axis=1)
    out = up[:, 0:L] * w[0]
    for k in range(1, K):
        out = out + up[:, k:k + L] * w[k]
    return out, up[:, -(K - 1):]


def real_linear_scan(a, b, h0):
    def comb(e1, e2):
        a1, b1 = e1
        a2, b2 = e2
        return a1 * a2, a2 * b1 + b2
    A, B = lax.associative_scan(comb, (a, b), axis=1)
    h = A * h0[:, None] + B
    return h, h[:, -1]


def complex_linear_scan(a_re, a_im, b_re, b_im, h0_re, h0_im):
    def comb(e1, e2):
        ar1, ai1, br1, bi1 = e1
        ar2, ai2, br2, bi2 = e2
        return (ar2 * ar1 - ai2 * ai1, ar2 * ai1 + ai2 * ar1,
                ar2 * br1 - ai2 * bi1 + br2, ar2 * bi1 + ai2 * br1 + bi2)
    Ar, Ai, Br, Bi = lax.associative_scan(comb, (a_re, a_im, b_re, b_im), axis=1)
    h0r = h0_re[:, None]
    h0i = h0_im[:, None]
    h_re = Ar * h0r - Ai * h0i + Br
    h_im = Ar * h0i + Ai * h0r + Bi
    return h_re, h_im


def mixer(x, lru_conv_buf, lru_h, sc_buf, ssm_re, ssm_im, p):
    bsz, L, _ = x.shape
    dt_ = x.dtype
    z = x @ p["w_in"]
    xa, ya, sb, sc, sh, us, gl = jnp.split(z, SPLITS, axis=-1)

    xa_c, new_lru_conv = causal_conv(xa, lru_conv_buf, p["conv_a_w"])
    xa_c = xa_c + p["conv_a_b"]
    xh = xa_c.reshape(bsz, L, LRU_HEADS, LRU_HEAD_DIM)
    gx = jax.nn.sigmoid(jnp.einsum('blhi,hij->blhj', xh, p["gate_x_w"]).reshape(bsz, L, LRU_WIDTH) + p["gate_x_b"])
    ga = jax.nn.sigmoid(jnp.einsum('blhi,hij->blhj', xh, p["gate_a_w"]).reshape(bsz, L, LRU_WIDTH) + p["gate_a_b"])
    log_a = -LRU_C * ga.astype(F32) * jax.nn.softplus(-p["lru_lambda"].astype(F32))
    a = jnp.exp(log_a)
    bx = jnp.sqrt(-jnp.expm1(2.0 * log_a)) * (gx * xa_c).astype(F32)
    h, h_last = real_linear_scan(a, bx, lru_h.astype(F32))
    out_a = (h.astype(dt_) * jax.nn.gelu(ya)) @ p["proj_a"]

    u = sc * sh
    cu, new_sc = causal_conv(u, sc_buf, p["conv_b_w"])
    out_b = (sb * cu) @ p["proj_b"]

    a_re = p["ssm_a_re"].astype(F32)
    a_im = p["ssm_a_im"].astype(F32)
    step = jnp.exp(p["ssm_log_dt"].astype(F32))
    mag = jnp.exp(step * a_re)
    abar_re = mag * jnp.cos(step * a_im)
    abar_im = mag * jnp.sin(step * a_im)
    den = a_re * a_re + a_im * a_im
    nr = abar_re - 1.0
    ni = abar_im
    coef_re = (nr * a_re + ni * a_im) / den
    coef_im = (ni * a_re - nr * a_im) / den
    b_re = p["ssm_b_re"].astype(F32)
    b_im = p["ssm_b_im"].astype(F32)
    bbar_re = coef_re[..., None] * b_re - coef_im[..., None] * b_im
    bbar_im = coef_re[..., None] * b_im + coef_im[..., None] * b_re
    uf = us.astype(F32)
    ug = uf.reshape(bsz, L, SSM_GROUPS, SSM_GROUP)
    bu_re = jnp.einsum('blgc,gpc->blgp', ug, bbar_re)
    bu_im = jnp.einsum('blgc,gpc->blgp', ug, bbar_im)
    h_re, h_im = complex_linear_scan(jnp.broadcast_to(abar_re, bu_re.shape),
                                     jnp.broadcast_to(abar_im, bu_im.shape),
                                     bu_re, bu_im, ssm_re.astype(F32), ssm_im.astype(F32))
    y = (jnp.einsum('blgp,gcp->blgc', h_re, p["ssm_c_re"].astype(F32))
         - jnp.einsum('blgp,gcp->blgc', h_im, p["ssm_c_im"].astype(F32)))
    y = y.reshape(bsz, L, SSM_WIDTH) + p["ssm_d"].astype(F32) * uf
    zc = jax.nn.gelu(y).astype(dt_)
    out_c = (zc * jax.nn.sigmoid(zc @ p["glu_w"] + p["glu_b"])) @ p["proj_c"]

    g = jax.nn.sigmoid(gl).reshape(bsz, L, N_BRANCH, D_MODEL)
    m = g[:, :, 0] * out_a + g[:, :, 1] * out_b + g[:, :, 2] * out_c
    out = m @ p["w_out"]
    new_states = (new_lru_conv.astype(lru_conv_buf.dtype), h_last.astype(lru_h.dtype),
                  new_sc.astype(sc_buf.dtype), h_re[:, -1].astype(ssm_re.dtype),
                  h_im[:, -1].astype(ssm_im.dtype))
    return out, new_states


def trunk_layer(x, lru_conv_buf, lru_h, sc_buf, ssm_re, ssm_im, p):
    m, st = mixer(x, lru_conv_buf, lru_h, sc_buf, ssm_re, ssm_im, p)
    x = layer_norm(ALPHA * x + m, p["ln1_g"], p["ln1_b"])
    f = jnp.square(jax.nn.relu(x @ p["mlp_up"])) @ p["mlp_down"]
    x = layer_norm(ALPHA * x + f, p["ln2_g"], p["ln2_b"])
    return x, st


def setup_inputs(seed: int = 0) -> dict:
    key = jax.random.key(seed)
    ks = jax.random.split(key, 40)
    nrm = lambda k, s, sc: jax.random.normal(k, s, F32) * sc
    rad = jax.random.uniform(ks[12], (DEPTH, LRU_WIDTH), F32, 0.9, 0.999)
    n_idx = jnp.arange(SSM_STATE, dtype=F32)
    lo, hi = math.log(0.001), math.log(0.1)
    return {
        "x_prompt": nrm(ks[0], (BATCH, SEQ, D_MODEL), 1.0),
        "x_sample": nrm(ks[1], (DEC_BATCH, DEC_SEQ, D_MODEL), 1.0),
        "state_lru_conv": nrm(ks[2], (DEPTH, DEC_BATCH, LRU_CONV - 1, LRU_WIDTH), 1.0),
        "state_lru_h": nrm(ks[3], (DEPTH, DEC_BATCH, LRU_WIDTH), 0.5),
        "state_sconv": nrm(ks[4], (DEPTH, DEC_BATCH, SC_CONV - 1, SC_WIDTH), 0.5),
        "state_ssm_re": nrm(ks[5], (DEPTH, DEC_BATCH, SSM_GROUPS, SSM_STATE), 0.3),
        "state_ssm_im": nrm(ks[6], (DEPTH, DEC_BATCH, SSM_GROUPS, SSM_STATE), 0.3),
        "w_in": nrm(ks[7], (DEPTH, D_MODEL, N_IN), D_MODEL ** -0.5),
        "conv_a_w": nrm(ks[8], (DEPTH, LRU_CONV, LRU_WIDTH), LRU_CONV ** -0.5),
        "conv_a_b": nrm(ks[9], (DEPTH, LRU_WIDTH), 0.01),
        "gate_x_w": nrm(ks[10], (DEPTH, LRU_HEADS, LRU_HEAD_DIM, LRU_HEAD_DIM), LRU_HEAD_DIM ** -0.5),
        "gate_x_b": nrm(ks[11], (DEPTH, LRU_WIDTH), 0.01),
        "gate_a_w": nrm(ks[13], (DEPTH, LRU_HEADS, LRU_HEAD_DIM, LRU_HEAD_DIM), LRU_HEAD_DIM ** -0.5),
        "gate_a_b": nrm(ks[14], (DEPTH, LRU_WIDTH), 0.01),
        "lru_lambda": jnp.log(rad) - jnp.log1p(-rad),
        "conv_b_w": nrm(ks[15], (DEPTH, SC_CONV, SC_WIDTH), SC_CONV ** -0.5),
        "ssm_a_re": -0.5 + nrm(ks[16], (DEPTH, SSM_GROUPS, SSM_STATE), 0.01),
        "ssm_a_im": math.pi * n_idx + nrm(ks[17], (DEPTH, SSM_GROUPS, SSM_STATE), 0.01),
        "ssm_log_dt": lo + (hi - lo) * jax.random.uniform(ks[18], (DEPTH, SSM_GROUPS, SSM_STATE), F32),
        "ssm_b_re": nrm(ks[19], (DEPTH, SSM_GROUPS, SSM_STATE, SSM_GROUP), (2 * SSM_GROUP) ** -0.5),
        "ssm_b_im": nrm(ks[20], (DEPTH, SSM_GROUPS, SSM_STATE, SSM_GROUP), (2 * SSM_GROUP) ** -0.5),
        "ssm_c_re": nrm(ks[21], (DEPTH, SSM_GROUPS, SSM_GROUP, SSM_STATE), (2 * SSM_STATE) ** -0.5),
        "ssm_c_im": nrm(ks[22], (DEPTH, SSM_GROUPS, SSM_GROUP, SSM_STATE), (2 * SSM_STATE) ** -0.5),
        "ssm_d": nrm(ks[23], (DEPTH, SSM_WIDTH), 1.0),
        "glu_w": nrm(ks[24], (DEPTH, SSM_WIDTH, SSM_WIDTH), SSM_WIDTH ** -0.5),
        "glu_b": nrm(ks[25], (DEPTH, SSM_WIDTH), 0.01),
        "proj_a": nrm(ks[26], (DEPTH, LRU_WIDTH, D_MODEL), LRU_WIDTH ** -0.5),
        "proj_b": nrm(ks[27], (DEPTH, SC_WIDTH, D_MODEL), SC_WIDTH ** -0.5),
        "proj_c": nrm(ks[28], (DEPTH, SSM_WIDTH, D_MODEL), SSM_WIDTH ** -0.5),
        "w_out": nrm(ks[29], (DEPTH, D_MODEL, D_MODEL), BETA * D_MODEL ** -0.5),
        "ln1_g": 1.0 + nrm(ks[30], (DEPTH, D_MODEL), 0.01),
        "ln1_b": nrm(ks[31], (DEPTH, D_MODEL), 0.01),
        "mlp_up": nrm(ks[32], (DEPTH, D_MODEL, D_FF), D_MODEL ** -0.5),
        "mlp_down": nrm(ks[33], (DEPTH, D_FF, D_MODEL), BETA * D_FF ** -0.5),
        "ln2_g": 1.0 + nrm(ks[34], (DEPTH, D_MODEL), 0.01),
        "ln2_b": nrm(ks[35], (DEPTH, D_MODEL), 0.01),
    }


def reference(x_prompt, x_sample, state_lru_conv, state_lru_h, state_sconv, state_ssm_re, state_ssm_im,
              w_in, conv_a_w, conv_a_b, gate_x_w, gate_x_b, gate_a_w, gate_a_b, lru_lambda, conv_b_w,
              ssm_a_re, ssm_a_im, ssm_log_dt, ssm_b_re, ssm_b_im, ssm_c_re, ssm_c_im, ssm_d, glu_w, glu_b,
              proj_a, proj_b, proj_c, w_out, ln1_g, ln1_b, mlp_up, mlp_down, ln2_g, ln2_b):
    dt_ = x_prompt.dtype
    z_lru_conv = jnp.zeros((BATCH, LRU_CONV - 1, LRU_WIDTH), dt_)
    z_lru_h = jnp.zeros((BATCH, LRU_WIDTH), dt_)
    z_sconv = jnp.zeros((BATCH, SC_CONV - 1, SC_WIDTH), dt_)
    z_ssm = jnp.zeros((BATCH, SSM_GROUPS, SSM_STATE), dt_)

    yp, ys = x_prompt, x_sample
    sp = [[] for _ in range(5)]
    ss = [[] for _ in range(5)]
    for i in range(DEPTH):
        p = {
            "w_in": w_in[i], "conv_a_w": conv_a_w[i], "conv_a_b": conv_a_b[i],
            "gate_x_w": gate_x_w[i], "gate_x_b": gate_x_b[i], "gate_a_w": gate_a_w[i],
            "gate_a_b": gate_a_b[i], "lru_lambda": lru_lambda[i], "conv_b_w": conv_b_w[i],
            "ssm_a_re": ssm_a_re[i], "ssm_a_im": ssm_a_im[i], "ssm_log_dt": ssm_log_dt[i],
            "ssm_b_re": ssm_b_re[i], "ssm_b_im": ssm_b_im[i], "ssm_c_re": ssm_c_re[i],
            "ssm_c_im": ssm_c_im[i], "ssm_d": ssm_d[i], "glu_w": glu_w[i], "glu_b": glu_b[i],
            "proj_a": proj_a[i], "proj_b": proj_b[i], "proj_c": proj_c[i], "w_out": w_out[i],
            "ln1_g": ln1_g[i], "ln1_b": ln1_b[i], "mlp_up": mlp_up[i], "mlp_down": mlp_down[i],
            "ln2_g": ln2_g[i], "ln2_b": ln2_b[i],
        }
        yp, stp = trunk_layer(yp, z_lru_conv, z_lru_h, z_sconv, z_ssm, z_ssm, p)
        ys, sts = trunk_layer(ys, state_lru_conv[i], state_lru_h[i], state_sconv[i],
                              state_ssm_re[i], state_ssm_im[i], p)
        for j in range(5):
            sp[j].append(stp[j])
            ss[j].append(sts[j])
    new_lru_conv_p = jnp.stack(sp[0], 0)
    new_lru_h_p = jnp.stack(sp[1], 0)
    new_sconv_p = jnp.stack(sp[2], 0)
    new_ssm_re_p = jnp.stack(sp[3], 0)
    new_ssm_im_p = jnp.stack(sp[4], 0)
    new_lru_conv_s = jnp.stack(ss[0], 0)
    new_lru_h_s = jnp.stack(ss[1], 0)
    new_sconv_s = jnp.stack(ss[2], 0)
    new_ssm_re_s = jnp.stack(ss[3], 0)
    new_ssm_im_s = jnp.stack(ss[4], 0)
    return (yp, ys, new_lru_conv_p, new_lru_h_p, new_sconv_p, new_ssm_re_p, new_ssm_im_p,
            new_lru_conv_s, new_lru_h_s, new_sconv_s, new_ssm_re_s, new_ssm_im_s)
```

```python
import functools

import jax
import jax.numpy as jnp
from jax import lax
from jax.experimental import pallas as pl
from jax.experimental.pallas import tpu as pltpu

F32 = jnp.float32
BF16 = jnp.bfloat16

D_MODEL = 1024
DEPTH = 2
WIDTH = 512
LRU_HEADS = 8
LRU_HEAD_DIM = WIDTH // LRU_HEADS
LRU_CONV = 4
LRU_C = 8.0
SC_CONV = 3
SSM_GROUP = 16
SSM_GROUPS = WIDTH // SSM_GROUP
SSM_STATE = 64
NS = SSM_GROUPS * SSM_STATE
D_FF = 4 * D_MODEL
N_IN = 6 * WIDTH + 3 * D_MODEL
ALPHA = (2 * DEPTH) ** 0.25
LN_EPS = 1e-5

OFF_XA, OFF_YA, OFF_SB, OFF_SC, OFF_SH, OFF_US, OFF_GL = 0, 512, 1024, 1536, 2048, 2560, 3072

ROW_TILE = 512
GATE_BLK = 256
B_BLK = 128
C_BLK = 512
FF_BLK = 1024

ROW_CAW, ROW_CAB, ROW_GXB, ROW_GAB, ROW_LAM, ROW_CBW, ROW_SSD, ROW_GLB = 0, 4, 5, 6, 7, 8, 11, 12

V7X_VMEM_LIMIT_BYTES = 58 * 1024 * 1024


def _dot(a, b):
    return jnp.dot(a, b, preferred_element_type=F32)


def _layer_norm(x, g, b):
    mu = jnp.mean(x, axis=-1, keepdims=True)
    xc = x - mu
    var = jnp.mean(xc * xc, axis=-1, keepdims=True)
    return xc * lax.rsqrt(var + LN_EPS) * g + b


def _s5_prep_kernel(a_re_ref, a_im_ref, log_dt_ref, b_re_ref, b_im_ref,
                    abar_re_ref, abar_im_ref, bbar_re_ref, bbar_im_ref):
    a_re = a_re_ref[...]
    a_im = a_im_ref[...]
    step = jnp.exp(log_dt_ref[...])
    mag = jnp.exp(step * a_re)
    abar_re = mag * jnp.cos(step * a_im)
    abar_im = mag * jnp.sin(step * a_im)
    den = a_re * a_re + a_im * a_im
    nr = abar_re - 1.0
    ni = abar_im
    coef_re = (nr * a_re + ni * a_im) / den
    coef_im = (ni * a_re - nr * a_im) / den
    b_re = b_re_ref[...]
    b_im = b_im_ref[...]
    abar_re_ref[...] = abar_re
    abar_im_ref[...] = abar_im
    bbar_re_ref[...] = coef_re * b_re - coef_im * b_im
    bbar_im_ref[...] = coef_re * b_im + coef_im * b_re


def _s5_prep(ssm_a_re, ssm_a_im, ssm_log_dt, ssm_b_re, ssm_b_im):
    flat = lambda a: a.reshape(DEPTH, 1, NS)
    chan_major = lambda b: jnp.transpose(b, (0, 3, 1, 2)).reshape(DEPTH, SSM_GROUP, NS)
    vec = jax.ShapeDtypeStruct((DEPTH, 1, NS), F32)
    mat = jax.ShapeDtypeStruct((DEPTH, SSM_GROUP, NS), F32)
    return pl.pallas_call(_s5_prep_kernel, out_shape=(vec, vec, mat, mat), name="s5_prep")(
        flat(ssm_a_re), flat(ssm_a_im), flat(ssm_log_dt), chan_major(ssm_b_re), chan_major(ssm_b_im))


def _mixer_kernel(R, T,
                  x_ref, ca_in, ha_in, cb_in, sre_in, sim_in,
                  w_in, gxw, gaw, pa, pb, pc, gluw, wout, bblk, cre, cim, vw, vln, vab,
                  x1_ref, ca_out, ha_out, cb_out, sre_out, sim_out,
                  xa_ext, u_ext, a_buf, h_buf, us_buf, bu, m_acc, ha, sre, sim):
    TR = R * T
    KA = (LRU_CONV - 1) * R
    KB = (SC_CONV - 1) * R
    i = pl.program_id(0)

    @pl.when(i == 0)
    def _():
        xa_ext[0:KA, :] = ca_in[...]
        u_ext[0:KB, :] = cb_in[...]
        ha[...] = ha_in[...]
        sre[...] = sre_in[...]
        sim[...] = sim_in[...]

    xb = x_ref[...].astype(BF16)

    def proj(lo, hi):
        return _dot(xb, w_in[:, lo:hi])

    def vrow(r, sl=slice(None)):
        return vw[r:r + 1, sl]

    za = proj(OFF_XA, OFF_SB)
    xa_ext[KA:KA + TR, :] = za[:, :WIDTH]
    gelu_ya = jax.nn.gelu(za[:, WIDTH:])
    xa_c = xa_ext[0:TR, :] * vrow(ROW_CAW)
    for k in range(1, LRU_CONV):
        xa_c = xa_c + xa_ext[k * R:k * R + TR, :] * vrow(ROW_CAW + k)
    xa_c = xa_c + vrow(ROW_CAB)
    softplus_neg_lam = jax.nn.softplus(-vrow(ROW_LAM))
    for hh in range(WIDTH // GATE_BLK):
        sl = slice(hh * GATE_BLK, (hh + 1) * GATE_BLK)
        xc = xa_c[:, sl]
        xcb = xc.astype(BF16)
        gx = jax.nn.sigmoid(_dot(xcb, gxw[hh]) + vrow(ROW_GXB, sl))
        ga = jax.nn.sigmoid(_dot(xcb, gaw[hh]) + vrow(ROW_GAB, sl))
        a = jnp.exp(-LRU_C * ga * softplus_neg_lam[:, sl])
        h_buf[:, sl] = jnp.sqrt(1.0 - a * a) * (gx * xc)
        a_buf[:, sl] = a

    def lru_step(t, h):
        rows = pl.ds(pl.multiple_of(t * R, R), R)
        h = a_buf[rows, :] * h + h_buf[rows, :]
        h_buf[rows, :] = h
        return h

    ha[...] = lax.fori_loop(0, T, lru_step, ha[...], unroll=min(T, 8))
    out_a = _dot((h_buf[...] * gelu_ya).astype(BF16), pa[...])
    m_acc[...] = jax.nn.sigmoid(proj(OFF_GL, OFF_GL + D_MODEL)) * out_a

    zb = proj(OFF_SB, OFF_US)
    u_ext[KB:KB + TR, :] = zb[:, WIDTH:2 * WIDTH] * zb[:, 2 * WIDTH:]
    cu = u_ext[0:TR, :] * vrow(ROW_CBW)
    for k in range(1, SC_CONV):
        cu = cu + u_ext[k * R:k * R + TR, :] * vrow(ROW_CBW + k)
    out_b = _dot((zb[:, :WIDTH] * cu).astype(BF16), pb[...])
    m_acc[...] += jax.nn.sigmoid(proj(OFF_GL + D_MODEL, OFF_GL + 2 * D_MODEL)) * out_b

    us_buf[...] = proj(OFF_US, OFF_GL)
    for m in range(WIDTH // B_BLK):
        usb = us_buf[:, m * B_BLK:(m + 1) * B_BLK].astype(BF16)
        bum = _dot(usb, bblk[m])
        bu[:, m * C_BLK:(m + 1) * C_BLK] = bum[:, :C_BLK]
        bu[:, NS + m * C_BLK:NS + (m + 1) * C_BLK] = bum[:, C_BLK:]

    slab = max(128, min(512, 4096 // R))
    for s in range(NS // slab):
        lo = s * slab
        ar = jnp.broadcast_to(vab[0:1, lo:lo + slab], (R, slab))
        ai = jnp.broadcast_to(vab[1:2, lo:lo + slab], (R, slab))

        def s5_step(t, carry, lo=lo, ar=ar, ai=ai):
            hr, hi = carry
            rows = pl.ds(pl.multiple_of(t * R, R), R)
            nr = ar * hr - ai * hi + bu[rows, lo:lo + slab]
            ni = ar * hi + ai * hr + bu[rows, NS + lo:NS + lo + slab]
            bu[rows, lo:lo + slab] = nr
            bu[rows, NS + lo:NS + lo + slab] = ni
            return nr, ni

        hr, hi = lax.fori_loop(0, T, s5_step, (sre[:, lo:lo + slab], sim[:, lo:lo + slab]),
                               unroll=min(T, 8))
        sre[:, lo:lo + slab] = hr
        sim[:, lo:lo + slab] = hi

    for k in range(NS // C_BLK):
        hre = bu[:, k * C_BLK:(k + 1) * C_BLK].astype(BF16)
        him = bu[:, NS + k * C_BLK:NS + (k + 1) * C_BLK].astype(BF16)
        sl = slice(k * 128, (k + 1) * 128)
        y = _dot(hre, cre[k]) + _dot(him, cim[k]) + vrow(ROW_SSD, sl) * us_buf[:, sl]
        us_buf[:, sl] = jax.nn.gelu(y)
    zc = us_buf[...]
    zcb = zc.astype(BF16)
    glu = zc * jax.nn.sigmoid(_dot(zcb, gluw[...]) + vrow(ROW_GLB))
    out_c = _dot(glu.astype(BF16), pc[...])
    m = m_acc[...] + jax.nn.sigmoid(proj(OFF_GL + 2 * D_MODEL, N_IN)) * out_c

    out = _dot(m.astype(BF16), wout[...])
    x1_ref[...] = _layer_norm(ALPHA * x_ref[...] + out, vln[0:1, :], vln[1:2, :])

    ca_tail = xa_ext[TR:TR + KA, :]
    cb_tail = u_ext[TR:TR + KB, :]
    xa_ext[0:KA, :] = ca_tail
    u_ext[0:KB, :] = cb_tail

    @pl.when(i == pl.num_programs(0) - 1)
    def _():
        ca_out[...] = ca_tail
        cb_out[...] = cb_tail
        ha_out[...] = ha[...]
        sre_out[...] = sre[...]
        sim_out[...] = sim[...]


def _resident(shape):
    return pl.BlockSpec(shape, lambda i: (0,) * len(shape), pipeline_mode=pl.Buffered(1))


def _mixer(x, states, wts, R):
    n_rows = x.shape[0]
    T = ROW_TILE // R
    grid = (n_rows // ROW_TILE,)
    row_spec = pl.BlockSpec((ROW_TILE, D_MODEL), lambda i: (i, 0))
    state_shapes = [((LRU_CONV - 1) * R, WIDTH), (R, WIDTH), ((SC_CONV - 1) * R, WIDTH), (R, NS), (R, NS)]
    in_specs = [row_spec] + [_resident(s) for s in state_shapes] + [_resident(w.shape) for w in wts]
    out_shape = [jax.ShapeDtypeStruct((n_rows, D_MODEL), F32)] + [jax.ShapeDtypeStruct(s, F32) for s in state_shapes]
    out_specs = [row_spec] + [pl.BlockSpec(s, lambda i: (0, 0)) for s in state_shapes]
    scratch = [
        pltpu.VMEM(((LRU_CONV - 1) * R + ROW_TILE, WIDTH), F32),
        pltpu.VMEM(((SC_CONV - 1) * R + ROW_TILE, WIDTH), F32),
        pltpu.VMEM((ROW_TILE, WIDTH), F32),
        pltpu.VMEM((ROW_TILE, WIDTH), F32),
        pltpu.VMEM((ROW_TILE, WIDTH), F32),
        pltpu.VMEM((ROW_TILE, 2 * NS), F32),
        pltpu.VMEM((ROW_TILE, D_MODEL), F32),
        pltpu.VMEM((R, WIDTH), F32),
        pltpu.VMEM((R, NS), F32),
        pltpu.VMEM((R, NS), F32),
    ]
    outs = pl.pallas_call(
        functools.partial(_mixer_kernel, R, T),
        grid=grid, in_specs=in_specs, out_specs=out_specs, out_shape=out_shape,
        scratch_shapes=scratch,
        compiler_params=pltpu.CompilerParams(dimension_semantics=("arbitrary",),
                                             vmem_limit_bytes=V7X_VMEM_LIMIT_BYTES),
        name=f"mixer_r{R}",
    )(x, *states, *wts)
    return outs[0], outs[1:]


def _mlp_kernel(x_ref, up, down, vln, o_ref):
    x = x_ref[...]
    xb = x.astype(BF16)
    acc = ALPHA * x
    for c in range(D_FF // FF_BLK):
        h = _dot(xb, up[:, c * FF_BLK:(c + 1) * FF_BLK])
        h = jnp.square(jnp.maximum(h, 0.0)).astype(BF16)
        acc = acc + _dot(h, down[c * FF_BLK:(c + 1) * FF_BLK, :])
    o_ref[...] = _layer_norm(acc, vln[0:1, :], vln[1:2, :])


def _mlp(x, up, down, vln):
    n_rows = x.shape[0]
    row_spec = pl.BlockSpec((ROW_TILE, D_MODEL), lambda i: (i, 0))
    return pl.pallas_call(
        _mlp_kernel,
        grid=(n_rows // ROW_TILE,),
        in_specs=[row_spec, _resident(up.shape), _resident(down.shape), _resident(vln.shape)],
        out_specs=row_spec,
        out_shape=jax.ShapeDtypeStruct((n_rows, D_MODEL), F32),
        compiler_params=pltpu.CompilerParams(dimension_semantics=("arbitrary",),
                                             vmem_limit_bytes=V7X_VMEM_LIMIT_BYTES),
        name="mlp",
    )(x, up, down, vln)


def _block_diag(blocks):
    n, a, b = blocks.shape[-3:]
    eye = jnp.eye(n, dtype=blocks.dtype)
    out = jnp.einsum("...nab,nm->...namb", blocks, eye)
    return out.reshape(*blocks.shape[:-3], n * a, n * b)


def _time_major(x):
    xt = jnp.swapaxes(x, 0, 1)
    return xt.reshape(xt.shape[0] * xt.shape[1], *xt.shape[2:])


def _batch_major(x, batch):
    return jnp.swapaxes(x.reshape(x.shape[0] // batch, batch, *x.shape[1:]), 0, 1)


def kernel(x_prompt, x_sample, state_lru_conv, state_lru_h, state_sconv, state_ssm_re, state_ssm_im,
           w_in, conv_a_w, conv_a_b, gate_x_w, gate_x_b, gate_a_w, gate_a_b, lru_lambda, conv_b_w,
           ssm_a_re, ssm_a_im, ssm_log_dt, ssm_b_re, ssm_b_im, ssm_c_re, ssm_c_im, ssm_d, glu_w, glu_b,
           proj_a, proj_b, proj_c, w_out, ln1_g, ln1_b, mlp_up, mlp_down, ln2_g, ln2_b):
    batch_p, batch_s = x_prompt.shape[0], x_sample.shape[0]
    bf = lambda w: w.astype(BF16)

    abar_re, abar_im, bbar_re, bbar_im = _s5_prep(ssm_a_re, ssm_a_im, ssm_log_dt, ssm_b_re, ssm_b_im)
    vab = jnp.concatenate([abar_re, abar_im], axis=1)

    def b_blocks(bbar):
        bb = bbar.reshape(DEPTH, SSM_GROUP, WIDTH // B_BLK, B_BLK // SSM_GROUP, SSM_STATE)
        return _block_diag(jnp.transpose(bb, (0, 2, 3, 1, 4)))

    bblk = bf(jnp.concatenate([b_blocks(bbar_re), b_blocks(bbar_im)], axis=-1))

    def c_blocks(c):
        cc = c.reshape(DEPTH, NS // C_BLK, C_BLK // SSM_STATE, SSM_GROUP, SSM_STATE)
        return bf(_block_diag(jnp.swapaxes(cc, -1, -2)))

    cre, cim = c_blocks(ssm_c_re), c_blocks(-ssm_c_im)

    def gate_blocks(w):
        heads = GATE_BLK // LRU_HEAD_DIM
        return bf(_block_diag(w.reshape(DEPTH, LRU_HEADS // heads, heads, LRU_HEAD_DIM, LRU_HEAD_DIM)))

    gxw, gaw = gate_blocks(gate_x_w), gate_blocks(gate_a_w)

    row = lambda v: v[:, None, :]
    vw = jnp.concatenate([conv_a_w, row(conv_a_b), row(gate_x_b), row(gate_a_b), row(lru_lambda),
                          conv_b_w, row(ssm_d), row(glu_b), jnp.zeros((DEPTH, 3, WIDTH), F32)], axis=1)
    vln1 = jnp.stack([ln1_g, ln1_b], axis=1)
    vln2 = jnp.stack([ln2_g, ln2_b], axis=1)
    w_in_b, pa, pb, pc = bf(w_in), bf(proj_a), bf(proj_b), bf(proj_c)
    gluw, wout, up, down = bf(glu_w), bf(w_out), bf(mlp_up), bf(mlp_down)

    def tails_in(s):
        st = jnp.swapaxes(s, 1, 2)
        return st.reshape(DEPTH, st.shape[1] * st.shape[2], WIDTH)

    def tails_out(s, batch):
        return jnp.swapaxes(s.reshape(DEPTH, s.shape[1] // batch, batch, WIDTH), 1, 2)

    dt = x_prompt.dtype
    states_s = (tails_in(state_lru_conv), state_lru_h, tails_in(state_sconv),
                state_ssm_re.reshape(DEPTH, batch_s, NS), state_ssm_im.reshape(DEPTH, batch_s, NS))
    states_p = (jnp.zeros((DEPTH, (LRU_CONV - 1) * batch_p, WIDTH), dt), jnp.zeros((DEPTH, batch_p, WIDTH), dt),
                jnp.zeros((DEPTH, (SC_CONV - 1) * batch_p, WIDTH), dt),
                jnp.zeros((DEPTH, batch_p, NS), dt), jnp.zeros((DEPTH, batch_p, NS), dt))

    yp, ys = _time_major(x_prompt), _time_major(x_sample)
    new_p, new_s = [], []
    for l in range(DEPTH):
        wts = (w_in_b[l], gxw[l], gaw[l], pa[l], pb[l], pc[l], gluw[l], wout[l], bblk[l], cre[l], cim[l],
               vw[l], vln1[l], vab[l])
        yp, stp = _mixer(yp, [s[l] for s in states_p], wts, batch_p)
        ys, sts = _mixer(ys, [s[l] for s in states_s], wts, batch_s)
        yp = _mlp(yp, up[l], down[l], vln2[l])
        ys = _mlp(ys, up[l], down[l], vln2[l])
        new_p.append(stp)
        new_s.append(sts)

    def collect(new, batch):
        ca, ha, cb, sre, sim = (jnp.stack([st[j] for st in new], 0) for j in range(5))
        shape_ssm = (DEPTH, batch, SSM_GROUPS, SSM_STATE)
        return (tails_out(ca, batch), ha, tails_out(cb, batch), sre.reshape(shape_ssm), sim.reshape(shape_ssm))

    return (_batch_major(yp, batch_p), _batch_major(ys, batch_s)) + collect(new_p, batch_p) + collect(new_s, batch_s)
```

```python
import functools

import jax
import jax.numpy as jnp
from jax import lax
from jax.experimental import pallas as pl
from jax.experimental.pallas import tpu as pltpu

F32 = jnp.float32
BF16 = jnp.bfloat16

D_MODEL = 1024
DEPTH = 2
WIDTH = 512
LRU_HEADS = 8
LRU_HEAD_DIM = WIDTH // LRU_HEADS
LRU_CONV = 4
LRU_C = 8.0
SC_CONV = 3
SSM_GROUP = 16
SSM_GROUPS = WIDTH // SSM_GROUP
SSM_STATE = 64
NS = SSM_GROUPS * SSM_STATE
D_FF = 4 * D_MODEL
N_IN = 6 * WIDTH + 3 * D_MODEL
ALPHA = (2 * DEPTH) ** 0.25
LN_EPS = 1e-5

OFF_XA, OFF_YA, OFF_SB, OFF_SC, OFF_SH, OFF_US, OFF_GL = 0, 512, 1024, 1536, 2048, 2560, 3072

ROW_TILE = 512
GATE_BLK = 256
B_BLK = 128
C_BLK = 512
FF_BLK = 1024

ROW_CAW, ROW_CAB, ROW_GXB, ROW_GAB, ROW_LAM, ROW_CBW, ROW_SSD, ROW_GLB = 0, 4, 5, 6, 7, 8, 11, 12

V7X_VMEM_LIMIT_BYTES = 58 * 1024 * 1024


def _dot(a, b):
    return jnp.dot(a, b, preferred_element_type=F32)


def _layer_norm(x, g, b):
    mu = jnp.mean(x, axis=-1, keepdims=True)
    xc = x - mu
    var = jnp.mean(xc * xc, axis=-1, keepdims=True)
    return xc * lax.rsqrt(var + LN_EPS) * g + b


def _s5_prep_kernel(a_re_ref, a_im_ref, log_dt_ref, b_re_ref, b_im_ref,
                    abar_ref, bbar_re_ref, bbar_im_ref):
    a_re = a_re_ref[...]
    a_im = a_im_ref[...]
    step = jnp.exp(log_dt_ref[...])
    mag = jnp.exp(step * a_re)
    abar_re = mag * jnp.cos(step * a_im)
    abar_im = mag * jnp.sin(step * a_im)
    den = a_re * a_re + a_im * a_im
    nr = abar_re - 1.0
    ni = abar_im
    coef_re = (nr * a_re + ni * a_im) / den
    coef_im = (ni * a_re - nr * a_im) / den
    b_re = b_re_ref[...]
    b_im = b_im_ref[...]
    abar_ref[:, 0:1, :] = abar_re
    abar_ref[:, 1:2, :] = abar_im
    bbar_re_ref[...] = coef_re * b_re - coef_im * b_im
    bbar_im_ref[...] = coef_re * b_im + coef_im * b_re


def _s5_prep(ssm_a_re, ssm_a_im, ssm_log_dt, ssm_b_re, ssm_b_im):
    flat = lambda a: a.reshape(DEPTH, 1, NS)
    chan_major = lambda b: jnp.transpose(b, (0, 3, 1, 2)).reshape(DEPTH, SSM_GROUP, NS)
    vec = jax.ShapeDtypeStruct((DEPTH, 2, NS), F32)
    mat = jax.ShapeDtypeStruct((DEPTH, SSM_GROUP, NS), F32)
    return pl.pallas_call(_s5_prep_kernel, out_shape=(vec, mat, mat), name="s5_prep")(
        flat(ssm_a_re), flat(ssm_a_im), flat(ssm_log_dt), chan_major(ssm_b_re), chan_major(ssm_b_im))


def _mixer_chunk(R, x_ref, wts, x1_ref, scratch):
    w_in, gxw, gaw, pa, pb, pc, gluw, wout, bblk, cre, cim, vw, vln, vab = wts
    xa_ext, u_ext, a_buf, h_buf, us_buf, bu, m_acc, ha, sre, sim = scratch
    T = ROW_TILE // R
    TR = ROW_TILE
    KA = (LRU_CONV - 1) * R
    KB = (SC_CONV - 1) * R

    xb = x_ref[...].astype(BF16)

    def proj(lo, hi):
        return _dot(xb, w_in[:, lo:hi])

    def vrow(r, sl=slice(None)):
        return vw[r:r + 1, sl]

    za = proj(OFF_XA, OFF_SB)
    xa_ext[KA:KA + TR, :] = za[:, :WIDTH]
    gelu_ya = jax.nn.gelu(za[:, WIDTH:])
    xa_c = xa_ext[0:TR, :] * vrow(ROW_CAW)
    for k in range(1, LRU_CONV):
        xa_c = xa_c + xa_ext[k * R:k * R + TR, :] * vrow(ROW_CAW + k)
    xa_c = xa_c + vrow(ROW_CAB)
    softplus_neg_lam = jax.nn.softplus(-vrow(ROW_LAM))
    for hh in range(WIDTH // GATE_BLK):
        sl = slice(hh * GATE_BLK, (hh + 1) * GATE_BLK)
        xc = xa_c[:, sl]
        xcb = xc.astype(BF16)
        gx = jax.nn.sigmoid(_dot(xcb, gxw[hh]) + vrow(ROW_GXB, sl))
        ga = jax.nn.sigmoid(_dot(xcb, gaw[hh]) + vrow(ROW_GAB, sl))
        a = jnp.exp(-LRU_C * ga * softplus_neg_lam[:, sl])
        h_buf[:, sl] = jnp.sqrt(1.0 - a * a) * (gx * xc)
        a_buf[:, sl] = a

    def lru_step(t, h):
        rows = pl.ds(pl.multiple_of(t * R, R), R)
        h = a_buf[rows, :] * h + h_buf[rows, :]
        h_buf[rows, :] = h
        return h

    ha[0:R, :] = lax.fori_loop(0, T, lru_step, ha[0:R, :], unroll=min(T, 8))
    out_a = _dot((h_buf[...] * gelu_ya).astype(BF16), pa[...])
    m_acc[...] = jax.nn.sigmoid(proj(OFF_GL, OFF_GL + D_MODEL)) * out_a

    zb = proj(OFF_SB, OFF_US)
    u_ext[KB:KB + TR, :] = zb[:, WIDTH:2 * WIDTH] * zb[:, 2 * WIDTH:]
    cu = u_ext[0:TR, :] * vrow(ROW_CBW)
    for k in range(1, SC_CONV):
        cu = cu + u_ext[k * R:k * R + TR, :] * vrow(ROW_CBW + k)
    out_b = _dot((zb[:, :WIDTH] * cu).astype(BF16), pb[...])
    m_acc[...] += jax.nn.sigmoid(proj(OFF_GL + D_MODEL, OFF_GL + 2 * D_MODEL)) * out_b

    us_buf[...] = proj(OFF_US, OFF_GL)
    for m in range(WIDTH // B_BLK):
        usb = us_buf[:, m * B_BLK:(m + 1) * B_BLK].astype(BF16)
        bum = _dot(usb, bblk[m])
        bu[:, m * C_BLK:(m + 1) * C_BLK] = bum[:, :C_BLK]
        bu[:, NS + m * C_BLK:NS + (m + 1) * C_BLK] = bum[:, C_BLK:]

    slab = max(128, min(512, 4096 // R))
    for s in range(NS // slab):
        lo = s * slab
        ar = jnp.broadcast_to(vab[0:1, lo:lo + slab], (R, slab))
        ai = jnp.broadcast_to(vab[1:2, lo:lo + slab], (R, slab))

        def s5_step(t, carry, lo=lo, ar=ar, ai=ai):
            hr, hi = carry
            rows = pl.ds(pl.multiple_of(t * R, R), R)
            nr = ar * hr - ai * hi + bu[rows, lo:lo + slab]
            ni = ar * hi + ai * hr + bu[rows, NS + lo:NS + lo + slab]
            bu[rows, lo:lo + slab] = nr
            bu[rows, NS + lo:NS + lo + slab] = ni
            return nr, ni

        hr, hi = lax.fori_loop(0, T, s5_step, (sre[0:R, lo:lo + slab], sim[0:R, lo:lo + slab]),
                               unroll=min(T, 8))
        sre[0:R, lo:lo + slab] = hr
        sim[0:R, lo:lo + slab] = hi

    for k in range(NS // C_BLK):
        hre = bu[:, k * C_BLK:(k + 1) * C_BLK].astype(BF16)
        him = bu[:, NS + k * C_BLK:NS + (k + 1) * C_BLK].astype(BF16)
        sl = slice(k * 128, (k + 1) * 128)
        y = _dot(hre, cre[k]) + _dot(him, cim[k]) + vrow(ROW_SSD, sl) * us_buf[:, sl]
        us_buf[:, sl] = jax.nn.gelu(y)
    zc = us_buf[...]
    glu = zc * jax.nn.sigmoid(_dot(zc.astype(BF16), gluw[...]) + vrow(ROW_GLB))
    out_c = _dot(glu.astype(BF16), pc[...])
    m = m_acc[...] + jax.nn.sigmoid(proj(OFF_GL + 2 * D_MODEL, N_IN)) * out_c

    out = _dot(m.astype(BF16), wout[...])
    x1_ref[...] = _layer_norm(ALPHA * x_ref[...] + out, vln[0:1, :], vln[1:2, :])

    xa_ext[0:KA, :] = xa_ext[TR:TR + KA, :]
    u_ext[0:KB, :] = u_ext[TR:TR + KB, :]


def _store_states(R, scratch, outs):
    xa_ext, u_ext, _, _, _, _, _, ha, sre, sim = scratch
    ca_out, ha_out, cb_out, sre_out, sim_out = outs
    for k in range(LRU_CONV - 1):
        ca_out[:, k * WIDTH:(k + 1) * WIDTH] = xa_ext[k * R:(k + 1) * R, :]
    for k in range(SC_CONV - 1):
        cb_out[:, k * WIDTH:(k + 1) * WIDTH] = u_ext[k * R:(k + 1) * R, :]
    ha_out[...] = ha[0:R, :]
    sre_out[...] = sre[0:R, :]
    sim_out[...] = sim[0:R, :]


def _mixer_kernel(n_prompt, r_prompt, r_sample, x_ref, *refs):
    states_s = refs[0:5]
    wts = refs[5:19]
    x1_ref = refs[19]
    outs_p, outs_s = refs[20:25], refs[25:30]
    scratch = refs[30:]
    xa_ext, u_ext, _, _, _, _, _, ha, sre, sim = scratch
    i = pl.program_id(0)

    @pl.when(i == 0)
    def _():
        xa_ext[0:(LRU_CONV - 1) * r_prompt, :] = jnp.zeros(((LRU_CONV - 1) * r_prompt, WIDTH), F32)
        u_ext[0:(SC_CONV - 1) * r_prompt, :] = jnp.zeros(((SC_CONV - 1) * r_prompt, WIDTH), F32)
        ha[0:r_prompt, :] = jnp.zeros((r_prompt, WIDTH), F32)
        sre[0:r_prompt, :] = jnp.zeros((r_prompt, NS), F32)
        sim[0:r_prompt, :] = jnp.zeros((r_prompt, NS), F32)

    @pl.when(i < n_prompt)
    def _():
        _mixer_chunk(r_prompt, x_ref, wts, x1_ref, scratch)

    @pl.when(i == n_prompt - 1)
    def _():
        _store_states(r_prompt, scratch, outs_p)

    @pl.when(i == n_prompt)
    def _():
        ca_in, ha_in, cb_in, sre_in, sim_in = states_s
        for k in range(LRU_CONV - 1):
            xa_ext[k * r_sample:(k + 1) * r_sample, :] = ca_in[:, k * WIDTH:(k + 1) * WIDTH]
        for k in range(SC_CONV - 1):
            u_ext[k * r_sample:(k + 1) * r_sample, :] = cb_in[:, k * WIDTH:(k + 1) * WIDTH]
        ha[0:r_sample, :] = ha_in[...]
        sre[0:r_sample, :] = sre_in[...]
        sim[0:r_sample, :] = sim_in[...]
        _mixer_chunk(r_sample, x_ref, wts, x1_ref, scratch)
        _store_states(r_sample, scratch, outs_s)


def _layer_block(layer, shape):
    zeros = (0,) * (len(shape) - 1)
    return pl.BlockSpec((None,) + tuple(shape[1:]), lambda i: (layer,) + zeros, pipeline_mode=pl.Buffered(1))


def _mixer(x, layer, states_s, wts, n_prompt_rows, r_prompt, r_sample):
    n_rows = x.shape[0]
    n_prompt = n_prompt_rows // ROW_TILE
    assert n_rows == n_prompt_rows + ROW_TILE and ROW_TILE % r_prompt == 0 and ROW_TILE % r_sample == 0
    row_spec = pl.BlockSpec((ROW_TILE, D_MODEL), lambda i: (i, 0))
    state_widths = [(LRU_CONV - 1) * WIDTH, WIDTH, (SC_CONV - 1) * WIDTH, NS, NS]
    state_shapes = [(r, w) for r in (r_prompt, r_sample) for w in state_widths]
    in_specs = [row_spec] + [_layer_block(layer, s.shape) for s in states_s] + [_layer_block(layer, w.shape) for w in wts]
    out_shape = [jax.ShapeDtypeStruct((n_rows, D_MODEL), F32)] + [jax.ShapeDtypeStruct(s, F32) for s in state_shapes]
    out_specs = [row_spec] + [pl.BlockSpec(s, lambda i: (0, 0)) for s in state_shapes]
    r_max = max(r_prompt, r_sample)
    scratch = [
        pltpu.VMEM(((LRU_CONV - 1) * r_max + ROW_TILE, WIDTH), F32),
        pltpu.VMEM(((SC_CONV - 1) * r_max + ROW_TILE, WIDTH), F32),
        pltpu.VMEM((ROW_TILE, WIDTH), F32),
        pltpu.VMEM((ROW_TILE, WIDTH), F32),
        pltpu.VMEM((ROW_TILE, WIDTH), F32),
        pltpu.VMEM((ROW_TILE, 2 * NS), F32),
        pltpu.VMEM((ROW_TILE, D_MODEL), F32),
        pltpu.VMEM((r_max, WIDTH), F32),
        pltpu.VMEM((r_max, NS), F32),
        pltpu.VMEM((r_max, NS), F32),
    ]
    outs = pl.pallas_call(
        functools.partial(_mixer_kernel, n_prompt, r_prompt, r_sample),
        grid=(n_prompt + 1,), in_specs=in_specs, out_specs=out_specs, out_shape=out_shape,
        scratch_shapes=scratch,
        compiler_params=pltpu.CompilerParams(dimension_semantics=("arbitrary",),
                                             vmem_limit_bytes=V7X_VMEM_LIMIT_BYTES),
        name="mixer",
    )(x, *states_s, *wts)
    return outs[0], outs[1:6], outs[6:11]


def _mlp_kernel(x_ref, up, down, vln, o_ref):
    x = x_ref[...]
    xb = x.astype(BF16)
    acc = ALPHA * x
    for c in range(D_FF // FF_BLK):
        h = _dot(xb, up[:, c * FF_BLK:(c + 1) * FF_BLK])
        h = jnp.square(jnp.maximum(h, 0.0)).astype(BF16)
        acc = acc + _dot(h, down[c * FF_BLK:(c + 1) * FF_BLK, :])
    o_ref[...] = _layer_norm(acc, vln[0:1, :], vln[1:2, :])


def _mlp(x, layer, up, down, vln):
    n_rows = x.shape[0]
    row_spec = pl.BlockSpec((ROW_TILE, D_MODEL), lambda i: (i, 0))
    return pl.pallas_call(
        _mlp_kernel,
        grid=(n_rows // ROW_TILE,),
        in_specs=[row_spec] + [_layer_block(layer, w.shape) for w in (up, down, vln)],
        out_specs=row_spec,
        out_shape=jax.ShapeDtypeStruct((n_rows, D_MODEL), F32),
        compiler_params=pltpu.CompilerParams(dimension_semantics=("arbitrary",),
                                             vmem_limit_bytes=V7X_VMEM_LIMIT_BYTES),
        name="mlp",
    )(x, up, down, vln)


def _block_diag(blocks):
    n, a, b = blocks.shape[-3:]
    eye = jnp.eye(n, dtype=blocks.dtype)
    out = jnp.einsum("...nab,nm->...namb", blocks, eye)
    return out.reshape(*blocks.shape[:-3], n * a, n * b)


def _time_major(x):
    xt = jnp.swapaxes(x, 0, 1)
    return xt.reshape(xt.shape[0] * xt.shape[1], *xt.shape[2:])


def _batch_major(x, batch):
    return jnp.swapaxes(x.reshape(x.shape[0] // batch, batch, *x.shape[1:]), 0, 1)


def kernel(x_prompt, x_sample, state_lru_conv, state_lru_h, state_sconv, state_ssm_re, state_ssm_im,
           w_in, conv_a_w, conv_a_b, gate_x_w, gate_x_b, gate_a_w, gate_a_b, lru_lambda, conv_b_w,
           ssm_a_re, ssm_a_im, ssm_log_dt, ssm_b_re, ssm_b_im, ssm_c_re, ssm_c_im, ssm_d, glu_w, glu_b,
           proj_a, proj_b, proj_c, w_out, ln1_g, ln1_b, mlp_up, mlp_down, ln2_g, ln2_b):
    batch_p, batch_s = x_prompt.shape[0], x_sample.shape[0]
    n_prompt_rows = batch_p * x_prompt.shape[1]
    assert batch_s * x_sample.shape[1] == ROW_TILE
    bf = lambda w: w.astype(BF16)

    vab, bbar_re, bbar_im = _s5_prep(ssm_a_re, ssm_a_im, ssm_log_dt, ssm_b_re, ssm_b_im)

    def b_blocks(bbar):
        bb = bbar.reshape(DEPTH, SSM_GROUP, WIDTH // B_BLK, B_BLK // SSM_GROUP, SSM_STATE)
        return _block_diag(jnp.transpose(bb, (0, 2, 3, 1, 4)))

    bblk = bf(jnp.concatenate([b_blocks(bbar_re), b_blocks(bbar_im)], axis=-1))

    def c_blocks(c):
        cc = c.reshape(DEPTH, NS // C_BLK, C_BLK // SSM_STATE, SSM_GROUP, SSM_STATE)
        return bf(_block_diag(jnp.swapaxes(cc, -1, -2)))

    cre, cim = c_blocks(ssm_c_re), c_blocks(-ssm_c_im)

    def gate_blocks(w):
        heads = GATE_BLK // LRU_HEAD_DIM
        return bf(_block_diag(w.reshape(DEPTH, LRU_HEADS // heads, heads, LRU_HEAD_DIM, LRU_HEAD_DIM)))

    gxw, gaw = gate_blocks(gate_x_w), gate_blocks(gate_a_w)

    row = lambda v: v[:, None, :]
    vw = jnp.concatenate([conv_a_w, row(conv_a_b), row(gate_x_b), row(gate_a_b), row(lru_lambda),
                          conv_b_w, row(ssm_d), row(glu_b), jnp.zeros((DEPTH, 3, WIDTH), F32)], axis=1)
    vln1 = jnp.stack([ln1_g, ln1_b], axis=1)
    vln2 = jnp.stack([ln2_g, ln2_b], axis=1)
    wts = (bf(w_in), gxw, gaw, bf(proj_a), bf(proj_b), bf(proj_c), bf(glu_w), bf(w_out), bblk, cre, cim,
           vw, vln1, vab)
    up, down = bf(mlp_up), bf(mlp_down)

    states_s = (state_lru_conv.reshape(DEPTH, batch_s, (LRU_CONV - 1) * WIDTH), state_lru_h,
                state_sconv.reshape(DEPTH, batch_s, (SC_CONV - 1) * WIDTH),
                state_ssm_re.reshape(DEPTH, batch_s, NS), state_ssm_im.reshape(DEPTH, batch_s, NS))

    y = jnp.concatenate([_time_major(x_prompt), _time_major(x_sample)], axis=0)
    new_p, new_s = [], []
    for layer in range(DEPTH):
        y, stp, sts = _mixer(y, layer, states_s, wts, n_prompt_rows, batch_p, batch_s)
        y = _mlp(y, layer, up, down, vln2)
        new_p.append(stp)
        new_s.append(sts)

    def collect(new, batch):
        ca, ha, cb, sre, sim = (jnp.stack([st[j] for st in new], 0) for j in range(5))
        shape_ssm = (DEPTH, batch, SSM_GROUPS, SSM_STATE)
        return (ca.reshape(DEPTH, batch, LRU_CONV - 1, WIDTH), ha, cb.reshape(DEPTH, batch, SC_CONV - 1, WIDTH),
                sre.reshape(shape_ssm), sim.reshape(shape_ssm))

    return ((_batch_major(y[:n_prompt_rows], batch_p), _batch_major(y[n_prompt_rows:], batch_s))
            + collect(new_p, batch_p) + collect(new_s, batch_s))
```

```python
import functools

import jax
import jax.numpy as jnp
from jax import lax
from jax.experimental import pallas as pl
from jax.experimental.pallas import tpu as pltpu

F32 = jnp.float32
BF16 = jnp.bfloat16

D_MODEL = 1024
DEPTH = 2
WIDTH = 512
LRU_HEADS = 8
LRU_HEAD_DIM = WIDTH // LRU_HEADS
LRU_CONV = 4
LRU_C = 8.0
SC_CONV = 3
SSM_GROUP = 16
SSM_GROUPS = WIDTH // SSM_GROUP
SSM_STATE = 64
NS = SSM_GROUPS * SSM_STATE
D_FF = 4 * D_MODEL
N_IN = 6 * WIDTH + 3 * D_MODEL
ALPHA = (2 * DEPTH) ** 0.25
LN_EPS = 1e-5

OFF_XA, OFF_YA, OFF_SB, OFF_SC, OFF_SH, OFF_US, OFF_GL = 0, 512, 1024, 1536, 2048, 2560, 3072

ROW_TILE = 512
GATE_BLK = 256
B_BLK = 128
C_BLK = 512
FF_BLK = 1024

ROW_CAW, ROW_CAB, ROW_GXB, ROW_GAB, ROW_LAM, ROW_CBW, ROW_SSD, ROW_GLB = 0, 4, 5, 6, 7, 8, 11, 12

V7X_VMEM_LIMIT_BYTES = 58 * 1024 * 1024


def _dot(a, b):
    return jnp.dot(a, b, preferred_element_type=F32)


def _layer_norm(x, g, b):
    mu = jnp.mean(x, axis=-1, keepdims=True)
    xc = x - mu
    var = jnp.mean(xc * xc, axis=-1, keepdims=True)
    return xc * lax.rsqrt(var + LN_EPS) * g + b


def _load_rows(x_ref):
    steps = x_ref.shape[1] // D_MODEL
    return jnp.concatenate([x_ref[:, t * D_MODEL:(t + 1) * D_MODEL] for t in range(steps)], axis=0)


def _store_rows(o_ref, rows):
    r = o_ref.shape[0]
    for t in range(o_ref.shape[1] // D_MODEL):
        o_ref[:, t * D_MODEL:(t + 1) * D_MODEL] = rows[t * r:(t + 1) * r, :]


def _s5_prep_kernel(a_re_ref, a_im_ref, log_dt_ref, b_re_ref, b_im_ref,
                    abar_ref, bbar_re_ref, bbar_im_ref):
    a_re = a_re_ref[...]
    a_im = a_im_ref[...]
    step = jnp.exp(log_dt_ref[...])
    mag = jnp.exp(step * a_re)
    abar_re = mag * jnp.cos(step * a_im)
    abar_im = mag * jnp.sin(step * a_im)
    den = a_re * a_re + a_im * a_im
    nr = abar_re - 1.0
    ni = abar_im
    coef_re = (nr * a_re + ni * a_im) / den
    coef_im = (ni * a_re - nr * a_im) / den
    b_re = b_re_ref[...]
    b_im = b_im_ref[...]
    abar_ref[:, 0:1, :] = abar_re
    abar_ref[:, 1:2, :] = abar_im
    bbar_re_ref[...] = coef_re * b_re - coef_im * b_im
    bbar_im_ref[...] = coef_re * b_im + coef_im * b_re


def _s5_prep(ssm_a_re, ssm_a_im, ssm_log_dt, ssm_b_re, ssm_b_im):
    flat = lambda a: a.reshape(DEPTH, 1, NS)
    chan_major = lambda b: jnp.transpose(b, (0, 3, 1, 2)).reshape(DEPTH, SSM_GROUP, NS)
    vec = jax.ShapeDtypeStruct((DEPTH, 2, NS), F32)
    mat = jax.ShapeDtypeStruct((DEPTH, SSM_GROUP, NS), F32)
    return pl.pallas_call(_s5_prep_kernel, out_shape=(vec, mat, mat), name="s5_prep")(
        flat(ssm_a_re), flat(ssm_a_im), flat(ssm_log_dt), chan_major(ssm_b_re), chan_major(ssm_b_im))


def _mixer_kernel(R, has_init, *refs):
    n_state = 5 if has_init else 0
    x_ref = refs[0]
    states_in = refs[1:1 + n_state]
    (w_in, gxw, gaw, pa, pb, pc, gluw, wout, bblk, cre, cim, vw, vln, vab) = refs[1 + n_state:15 + n_state]
    x1_ref, ca_out, ha_out, cb_out, sre_out, sim_out = refs[15 + n_state:21 + n_state]
    xa_ext, u_ext, a_buf, h_buf, us_buf, bu, m_acc, ha, sre, sim = refs[21 + n_state:]
    T = ROW_TILE // R
    TR = ROW_TILE
    KA = (LRU_CONV - 1) * R
    KB = (SC_CONV - 1) * R
    i = pl.program_id(0)

    @pl.when(i == 0)
    def _():
        if has_init:
            ca_in, ha_in, cb_in, sre_in, sim_in = states_in
            for k in range(LRU_CONV - 1):
                xa_ext[k * R:(k + 1) * R, :] = ca_in[:, k * WIDTH:(k + 1) * WIDTH]
            for k in range(SC_CONV - 1):
                u_ext[k * R:(k + 1) * R, :] = cb_in[:, k * WIDTH:(k + 1) * WIDTH]
            ha[...] = ha_in[...]
            sre[...] = sre_in[...]
            sim[...] = sim_in[...]
        else:
            xa_ext[0:KA, :] = jnp.zeros((KA, WIDTH), F32)
            u_ext[0:KB, :] = jnp.zeros((KB, WIDTH), F32)
            ha[...] = jnp.zeros((R, WIDTH), F32)
            sre[...] = jnp.zeros((R, NS), F32)
            sim[...] = jnp.zeros((R, NS), F32)

    xb = _load_rows(x_ref).astype(BF16)

    def proj(lo, hi):
        return _dot(xb, w_in[:, lo:hi])

    def vrow(r, sl=slice(None)):
        return vw[r:r + 1, sl]

    za = proj(OFF_XA, OFF_SB)
    xa_ext[KA:KA + TR, :] = za[:, :WIDTH]
    gelu_ya = jax.nn.gelu(za[:, WIDTH:])
    xa_c = xa_ext[0:TR, :] * vrow(ROW_CAW)
    for k in range(1, LRU_CONV):
        xa_c = xa_c + xa_ext[k * R:k * R + TR, :] * vrow(ROW_CAW + k)
    xa_c = xa_c + vrow(ROW_CAB)
    softplus_neg_lam = jax.nn.softplus(-vrow(ROW_LAM))
    for hh in range(WIDTH // GATE_BLK):
        sl = slice(hh * GATE_BLK, (hh + 1) * GATE_BLK)
        xc = xa_c[:, sl]
        xcb = xc.astype(BF16)
        gx = jax.nn.sigmoid(_dot(xcb, gxw[hh]) + vrow(ROW_GXB, sl))
        ga = jax.nn.sigmoid(_dot(xcb, gaw[hh]) + vrow(ROW_GAB, sl))
        a = jnp.exp(-LRU_C * ga * softplus_neg_lam[:, sl])
        h_buf[:, sl] = jnp.sqrt(1.0 - a * a) * (gx * xc)
        a_buf[:, sl] = a

    def lru_step(t, h):
        rows = pl.ds(pl.multiple_of(t * R, R), R)
        h = a_buf[rows, :] * h + h_buf[rows, :]
        h_buf[rows, :] = h
        return h

    ha[...] = lax.fori_loop(0, T, lru_step, ha[...], unroll=min(T, 8))
    out_a = _dot((h_buf[...] * gelu_ya).astype(BF16), pa[...])
    m_acc[...] = jax.nn.sigmoid(proj(OFF_GL, OFF_GL + D_MODEL)) * out_a

    zb = proj(OFF_SB, OFF_US)
    u_ext[KB:KB + TR, :] = zb[:, WIDTH:2 * WIDTH] * zb[:, 2 * WIDTH:]
    cu = u_ext[0:TR, :] * vrow(ROW_CBW)
    for k in range(1, SC_CONV):
        cu = cu + u_ext[k * R:k * R + TR, :] * vrow(ROW_CBW + k)
    out_b = _dot((zb[:, :WIDTH] * cu).astype(BF16), pb[...])
    m_acc[...] += jax.nn.sigmoid(proj(OFF_GL + D_MODEL, OFF_GL + 2 * D_MODEL)) * out_b

    us_buf[...] = proj(OFF_US, OFF_GL)
    for m in range(WIDTH // B_BLK):
        usb = us_buf[:, m * B_BLK:(m + 1) * B_BLK].astype(BF16)
        bum = _dot(usb, bblk[m])
        bu[:, m * C_BLK:(m + 1) * C_BLK] = bum[:, :C_BLK]
        bu[:, NS + m * C_BLK:NS + (m + 1) * C_BLK] = bum[:, C_BLK:]

    slab = max(128, min(512, 4096 // R))
    for s in range(NS // slab):
        lo = s * slab
        ar = jnp.broadcast_to(vab[0:1, lo:lo + slab], (R, slab))
        ai = jnp.broadcast_to(vab[1:2, lo:lo + slab], (R, slab))

        def s5_step(t, carry, lo=lo, ar=ar, ai=ai):
            hr, hi = carry
            rows = pl.ds(pl.multiple_of(t * R, R), R)
            nr = ar * hr - ai * hi + bu[rows, lo:lo + slab]
            ni = ar * hi + ai * hr + bu[rows, NS + lo:NS + lo + slab]
            bu[rows, lo:lo + slab] = nr
            bu[rows, NS + lo:NS + lo + slab] = ni
            return nr, ni

        hr, hi = lax.fori_loop(0, T, s5_step, (sre[:, lo:lo + slab], sim[:, lo:lo + slab]),
                               unroll=min(T, 8))
        sre[:, lo:lo + slab] = hr
        sim[:, lo:lo + slab] = hi

    for k in range(NS // C_BLK):
        hre = bu[:, k * C_BLK:(k + 1) * C_BLK].astype(BF16)
        him = bu[:, NS + k * C_BLK:NS + (k + 1) * C_BLK].astype(BF16)
        sl = slice(k * 128, (k + 1) * 128)
        y = _dot(hre, cre[k]) + _dot(him, cim[k]) + vrow(ROW_SSD, sl) * us_buf[:, sl]
        us_buf[:, sl] = jax.nn.gelu(y)
    zc = us_buf[...]
    glu = zc * jax.nn.sigmoid(_dot(zc.astype(BF16), gluw[...]) + vrow(ROW_GLB))
    out_c = _dot(glu.astype(BF16), pc[...])
    m = m_acc[...] + jax.nn.sigmoid(proj(OFF_GL + 2 * D_MODEL, N_IN)) * out_c

    out = _dot(m.astype(BF16), wout[...])
    _store_rows(x1_ref, _layer_norm(ALPHA * _load_rows(x_ref) + out, vln[0:1, :], vln[1:2, :]))

    xa_ext[0:KA, :] = xa_ext[TR:TR + KA, :]
    u_ext[0:KB, :] = u_ext[TR:TR + KB, :]

    @pl.when(i == pl.num_programs(0) - 1)
    def _():
        for k in range(LRU_CONV - 1):
            ca_out[:, k * WIDTH:(k + 1) * WIDTH] = xa_ext[k * R:(k + 1) * R, :]
        for k in range(SC_CONV - 1):
            cb_out[:, k * WIDTH:(k + 1) * WIDTH] = u_ext[k * R:(k + 1) * R, :]
        ha_out[...] = ha[...]
        sre_out[...] = sre[...]
        sim_out[...] = sim[...]


def _layer_block(layer, shape):
    zeros = (0,) * (len(shape) - 1)
    return pl.BlockSpec((None,) + tuple(shape[1:]), lambda i: (layer,) + zeros, pipeline_mode=pl.Buffered(1))


def _mixer(x, layer, states, wts):
    R = x.shape[0]
    T = ROW_TILE // R
    n_steps = x.shape[1] // (T * D_MODEL)
    assert ROW_TILE % R == 0 and x.shape[1] % (T * D_MODEL) == 0
    x_spec = pl.BlockSpec((R, T * D_MODEL), lambda i: (0, i))
    state_shapes = [(R, (LRU_CONV - 1) * WIDTH), (R, WIDTH), (R, (SC_CONV - 1) * WIDTH), (R, NS), (R, NS)]
    states = () if states is None else tuple(states)
    in_specs = ([x_spec] + [_layer_block(layer, s.shape) for s in states]
                + [_layer_block(layer, w.shape) for w in wts])
    out_shape = [jax.ShapeDtypeStruct(x.shape, F32)] + [jax.ShapeDtypeStruct(s, F32) for s in state_shapes]
    out_specs = [x_spec] + [pl.BlockSpec(s, lambda i: (0, 0)) for s in state_shapes]
    scratch = [
        pltpu.VMEM(((LRU_CONV - 1) * R + ROW_TILE, WIDTH), F32),
        pltpu.VMEM(((SC_CONV - 1) * R + ROW_TILE, WIDTH), F32),
        pltpu.VMEM((ROW_TILE, WIDTH), F32),
        pltpu.VMEM((ROW_TILE, WIDTH), F32),
        pltpu.VMEM((ROW_TILE, WIDTH), F32),
        pltpu.VMEM((ROW_TILE, 2 * NS), F32),
        pltpu.VMEM((ROW_TILE, D_MODEL), F32),
        pltpu.VMEM((R, WIDTH), F32),
        pltpu.VMEM((R, NS), F32),
        pltpu.VMEM((R, NS), F32),
    ]
    outs = pl.pallas_call(
        functools.partial(_mixer_kernel, R, bool(states)),
        grid=(n_steps,), in_specs=in_specs, out_specs=out_specs, out_shape=out_shape,
        scratch_shapes=scratch,
        compiler_params=pltpu.CompilerParams(dimension_semantics=("arbitrary",),
                                             vmem_limit_bytes=V7X_VMEM_LIMIT_BYTES),
        name=f"mixer_r{R}",
    )(x, *states, *wts)
    return outs[0], outs[1:]


def _mlp_rows(x, up, down, vln):
    xb = x.astype(BF16)
    acc = ALPHA * x
    for c in range(D_FF // FF_BLK):
        h = _dot(xb, up[:, c * FF_BLK:(c + 1) * FF_BLK])
        h = jnp.square(jnp.maximum(h, 0.0)).astype(BF16)
        acc = acc + _dot(h, down[c * FF_BLK:(c + 1) * FF_BLK, :])
    return _layer_norm(acc, vln[0:1, :], vln[1:2, :])


def _mlp_kernel(n_prompt, xp_ref, xs_ref, up, down, vln, op_ref, os_ref):
    i = pl.program_id(0)

    @pl.when(i < n_prompt)
    def _():
        _store_rows(op_ref, _mlp_rows(_load_rows(xp_ref), up, down, vln))

    @pl.when(i == n_prompt)
    def _():
        _store_rows(os_ref, _mlp_rows(_load_rows(xs_ref), up, down, vln))


def _mlp(xp, xs, layer, up, down, vln):
    rp, rs = xp.shape[0], xs.shape[0]
    tp = ROW_TILE // rp
    n_prompt = xp.shape[1] // (tp * D_MODEL)
    assert xs.shape[1] * rs == ROW_TILE * D_MODEL
    p_spec = pl.BlockSpec((rp, tp * D_MODEL), lambda i: (0, jnp.minimum(i, n_prompt - 1)))
    s_spec = pl.BlockSpec(xs.shape, lambda i: (0, 0))
    return pl.pallas_call(
        functools.partial(_mlp_kernel, n_prompt),
        grid=(n_prompt + 1,),
        in_specs=[p_spec, s_spec] + [_layer_block(layer, w.shape) for w in (up, down, vln)],
        out_specs=[p_spec, s_spec],
        out_shape=[jax.ShapeDtypeStruct(xp.shape, F32), jax.ShapeDtypeStruct(xs.shape, F32)],
        compiler_params=pltpu.CompilerParams(dimension_semantics=("arbitrary",),
                                             vmem_limit_bytes=V7X_VMEM_LIMIT_BYTES),
        name="mlp",
    )(xp, xs, up, down, vln)


def _block_diag(blocks):
    n, a, b = blocks.shape[-3:]
    eye = jnp.eye(n, dtype=blocks.dtype)
    out = jnp.einsum("...nab,nm->...namb", blocks, eye)
    return out.reshape(*blocks.shape[:-3], n * a, n * b)


def kernel(x_prompt, x_sample, state_lru_conv, state_lru_h, state_sconv, state_ssm_re, state_ssm_im,
           w_in, conv_a_w, conv_a_b, gate_x_w, gate_x_b, gate_a_w, gate_a_b, lru_lambda, conv_b_w,
           ssm_a_re, ssm_a_im, ssm_log_dt, ssm_b_re, ssm_b_im, ssm_c_re, ssm_c_im, ssm_d, glu_w, glu_b,
           proj_a, proj_b, proj_c, w_out, ln1_g, ln1_b, mlp_up, mlp_down, ln2_g, ln2_b):
    batch_p, batch_s = x_prompt.shape[0], x_sample.shape[0]
    bf = lambda w: w.astype(BF16)

    vab, bbar_re, bbar_im = _s5_prep(ssm_a_re, ssm_a_im, ssm_log_dt, ssm_b_re, ssm_b_im)

    def b_blocks(bbar):
        bb = bbar.reshape(DEPTH, SSM_GROUP, WIDTH // B_BLK, B_BLK // SSM_GROUP, SSM_STATE)
        return _block_diag(jnp.transpose(bb, (0, 2, 3, 1, 4)))

    bblk = bf(jnp.concatenate([b_blocks(bbar_re), b_blocks(bbar_im)], axis=-1))

    def c_blocks(c):
        cc = c.reshape(DEPTH, NS // C_BLK, C_BLK // SSM_STATE, SSM_GROUP, SSM_STATE)
        return bf(_block_diag(jnp.swapaxes(cc, -1, -2)))

    cre, cim = c_blocks(ssm_c_re), c_blocks(-ssm_c_im)

    def gate_blocks(w):
        heads = GATE_BLK // LRU_HEAD_DIM
        return bf(_block_diag(w.reshape(DEPTH, LRU_HEADS // heads, heads, LRU_HEAD_DIM, LRU_HEAD_DIM)))

    gxw, gaw = gate_blocks(gate_x_w), gate_blocks(gate_a_w)

    row = lambda v: v[:, None, :]
    vw = jnp.concatenate([conv_a_w, row(conv_a_b), row(gate_x_b), row(gate_a_b), row(lru_lambda),
                          conv_b_w, row(ssm_d), row(glu_b), jnp.zeros((DEPTH, 3, WIDTH), F32)], axis=1)
    vln1 = jnp.stack([ln1_g, ln1_b], axis=1)
    vln2 = jnp.stack([ln2_g, ln2_b], axis=1)
    wts = (bf(w_in), gxw, gaw, bf(proj_a), bf(proj_b), bf(proj_c), bf(glu_w), bf(w_out), bblk, cre, cim,
           vw, vln1, vab)
    up, down = bf(mlp_up), bf(mlp_down)

    states_s = (state_lru_conv.reshape(DEPTH, batch_s, (LRU_CONV - 1) * WIDTH), state_lru_h,
                state_sconv.reshape(DEPTH, batch_s, (SC_CONV - 1) * WIDTH),
                state_ssm_re.reshape(DEPTH, batch_s, NS), state_ssm_im.reshape(DEPTH, batch_s, NS))

    yp = x_prompt.reshape(batch_p, -1)
    ys = x_sample.reshape(batch_s, -1)
    new_p, new_s = [], []
    for layer in range(DEPTH):
        yp, stp = _mixer(yp, layer, None, wts)
        ys, sts = _mixer(ys, layer, states_s, wts)
        yp, ys = _mlp(yp, ys, layer, up, down, vln2)
        new_p.append(stp)
        new_s.append(sts)

    def collect(new, batch):
        ca, ha, cb, sre, sim = (jnp.stack([st[j] for st in new], 0) for j in range(5))
        shape_ssm = (DEPTH, batch, SSM_GROUPS, SSM_STATE)
        return (ca.reshape(DEPTH, batch, LRU_CONV - 1, WIDTH), ha, cb.reshape(DEPTH, batch, SC_CONV - 1, WIDTH),
                sre.reshape(shape_ssm), sim.reshape(shape_ssm))

    return ((yp.reshape(x_prompt.shape), ys.reshape(x_sample.shape))
            + collect(new_p, batch_p) + collect(new_s, batch_s))
```

```python
import functools

import jax
import jax.numpy as jnp
from jax import lax
from jax.experimental import pallas as pl
from jax.experimental.pallas import tpu as pltpu

F32 = jnp.float32
BF16 = jnp.bfloat16

D_MODEL = 1024
DEPTH = 2
WIDTH = 512
LRU_HEADS = 8
LRU_HEAD_DIM = WIDTH // LRU_HEADS
LRU_CONV = 4
LRU_C = 8.0
SC_CONV = 3
SSM_GROUP = 16
SSM_GROUPS = WIDTH // SSM_GROUP
SSM_STATE = 64
NS = SSM_GROUPS * SSM_STATE
D_FF = 4 * D_MODEL
N_IN = 6 * WIDTH + 3 * D_MODEL
ALPHA = (2 * DEPTH) ** 0.25
LN_EPS = 1e-5

OFF_XA, OFF_YA, OFF_SB, OFF_SC, OFF_SH, OFF_US, OFF_GL = 0, 512, 1024, 1536, 2048, 2560, 3072

ROW_TILE = 512
GATE_BLK = 256
B_BLK = 128
C_BLK = 512
FF_BLK = 1024

ROW_CAW, ROW_CAB, ROW_GXB, ROW_GAB, ROW_LAM, ROW_CBW, ROW_SSD, ROW_GLB = 0, 4, 5, 6, 7, 8, 11, 12

V7X_VMEM_LIMIT_BYTES = 58 * 1024 * 1024


def _dot(a, b):
    return jnp.dot(a, b, preferred_element_type=F32)


def _layer_norm(x, g, b):
    mu = jnp.mean(x, axis=-1, keepdims=True)
    xc = x - mu
    var = jnp.mean(xc * xc, axis=-1, keepdims=True)
    return xc * lax.rsqrt(var + LN_EPS) * g + b


def _load_rows(x_ref):
    steps = x_ref.shape[1] // D_MODEL
    return jnp.concatenate([x_ref[:, t * D_MODEL:(t + 1) * D_MODEL] for t in range(steps)], axis=0)


def _store_rows(o_ref, rows):
    r = o_ref.shape[0]
    for t in range(o_ref.shape[1] // D_MODEL):
        o_ref[:, t * D_MODEL:(t + 1) * D_MODEL] = rows[t * r:(t + 1) * r, :]


def _s5_prep_kernel(a_re_ref, a_im_ref, log_dt_ref, b_re_ref, b_im_ref,
                    abar_ref, bbar_re_ref, bbar_im_ref):
    a_re = a_re_ref[...]
    a_im = a_im_ref[...]
    step = jnp.exp(log_dt_ref[...])
    mag = jnp.exp(step * a_re)
    abar_re = mag * jnp.cos(step * a_im)
    abar_im = mag * jnp.sin(step * a_im)
    den = a_re * a_re + a_im * a_im
    nr = abar_re - 1.0
    ni = abar_im
    coef_re = (nr * a_re + ni * a_im) / den
    coef_im = (ni * a_re - nr * a_im) / den
    b_re = b_re_ref[...]
    b_im = b_im_ref[...]
    abar_ref[:, 0:1, :] = abar_re
    abar_ref[:, 1:2, :] = abar_im
    bbar_re_ref[...] = coef_re * b_re - coef_im * b_im
    bbar_im_ref[...] = coef_re * b_im + coef_im * b_re


def _s5_prep(ssm_a_re, ssm_a_im, ssm_log_dt, ssm_b_re, ssm_b_im):
    flat = lambda a: a.reshape(DEPTH, 1, NS)
    chan_major = lambda b: jnp.transpose(b, (0, 3, 1, 2)).reshape(DEPTH, SSM_GROUP, NS)
    vec = jax.ShapeDtypeStruct((DEPTH, 2, NS), F32)
    mat = jax.ShapeDtypeStruct((DEPTH, SSM_GROUP, NS), F32)
    return pl.pallas_call(_s5_prep_kernel, out_shape=(vec, mat, mat), name="s5_prep")(
        flat(ssm_a_re), flat(ssm_a_im), flat(ssm_log_dt), chan_major(ssm_b_re), chan_major(ssm_b_im))


def _mixer_kernel(R, has_init, *refs):
    n_state = 5 if has_init else 0
    x_ref = refs[0]
    states_in = refs[1:1 + n_state]
    (w_in, gxw, gaw, pa, pb, pc, gluw, wout, bblk, cre, cim, vw, vln, vab) = refs[1 + n_state:15 + n_state]
    x1_ref, ca_out, ha_out, cb_out, sre_out, sim_out = refs[15 + n_state:21 + n_state]
    xa_ext, u_ext, a_buf, h_buf, us_buf, bu, m_acc, ha, sre, sim = refs[21 + n_state:]
    T = ROW_TILE // R
    TR = ROW_TILE
    KA = (LRU_CONV - 1) * R
    KB = (SC_CONV - 1) * R
    i = pl.program_id(0)

    @pl.when(i == 0)
    def _():
        if has_init:
            ca_in, ha_in, cb_in, sre_in, sim_in = states_in
            for k in range(LRU_CONV - 1):
                xa_ext[k * R:(k + 1) * R, :] = ca_in[:, k * WIDTH:(k + 1) * WIDTH]
            for k in range(SC_CONV - 1):
                u_ext[k * R:(k + 1) * R, :] = cb_in[:, k * WIDTH:(k + 1) * WIDTH]
            ha[...] = ha_in[...]
            sre[...] = sre_in[...]
            sim[...] = sim_in[...]
        else:
            xa_ext[0:KA, :] = jnp.zeros((KA, WIDTH), F32)
            u_ext[0:KB, :] = jnp.zeros((KB, WIDTH), F32)
            ha[...] = jnp.zeros((R, WIDTH), F32)
            sre[...] = jnp.zeros((R, NS), F32)
            sim[...] = jnp.zeros((R, NS), F32)

    xb = _load_rows(x_ref).astype(BF16)

    def proj(lo, hi):
        return _dot(xb, w_in[:, lo:hi])

    def vrow(r, sl=slice(None)):
        return vw[r:r + 1, sl]

    za = proj(OFF_XA, OFF_SB)
    xa_ext[KA:KA + TR, :] = za[:, :WIDTH]
    gelu_ya = jax.nn.gelu(za[:, WIDTH:])
    xa_c = xa_ext[0:TR, :] * vrow(ROW_CAW)
    for k in range(1, LRU_CONV):
        xa_c = xa_c + xa_ext[k * R:k * R + TR, :] * vrow(ROW_CAW + k)
    xa_c = xa_c + vrow(ROW_CAB)
    softplus_neg_lam = jax.nn.softplus(-vrow(ROW_LAM))
    for hh in range(WIDTH // GATE_BLK):
        sl = slice(hh * GATE_BLK, (hh + 1) * GATE_BLK)
        xc = xa_c[:, sl]
        xcb = xc.astype(BF16)
        gx = jax.nn.sigmoid(_dot(xcb, gxw[hh]) + vrow(ROW_GXB, sl))
        ga = jax.nn.sigmoid(_dot(xcb, gaw[hh]) + vrow(ROW_GAB, sl))
        a = jnp.exp(-LRU_C * ga * softplus_neg_lam[:, sl])
        h_buf[:, sl] = jnp.sqrt(1.0 - a * a) * (gx * xc)
        a_buf[:, sl] = a

    h = ha[...]
    for t in range(T):
        rows = slice(t * R, (t + 1) * R)
        h = a_buf[rows, :] * h + h_buf[rows, :]
        h_buf[rows, :] = h
    ha[...] = h
    out_a = _dot((h_buf[...] * gelu_ya).astype(BF16), pa[...])
    m_acc[...] = jax.nn.sigmoid(proj(OFF_GL, OFF_GL + D_MODEL)) * out_a

    zb = proj(OFF_SB, OFF_US)
    u_ext[KB:KB + TR, :] = zb[:, WIDTH:2 * WIDTH] * zb[:, 2 * WIDTH:]
    cu = u_ext[0:TR, :] * vrow(ROW_CBW)
    for k in range(1, SC_CONV):
        cu = cu + u_ext[k * R:k * R + TR, :] * vrow(ROW_CBW + k)
    out_b = _dot((zb[:, :WIDTH] * cu).astype(BF16), pb[...])
    m_acc[...] += jax.nn.sigmoid(proj(OFF_GL + D_MODEL, OFF_GL + 2 * D_MODEL)) * out_b

    us_buf[...] = proj(OFF_US, OFF_GL)
    for m in range(WIDTH // B_BLK):
        usb = us_buf[:, m * B_BLK:(m + 1) * B_BLK].astype(BF16)
        bum = _dot(usb, bblk[m])
        bu[:, m * C_BLK:(m + 1) * C_BLK] = bum[:, :C_BLK]
        bu[:, NS + m * C_BLK:NS + (m + 1) * C_BLK] = bum[:, C_BLK:]

    slab = max(128, min(512, 4096 // R))
    for s in range(NS // slab):
        lo = s * slab
        ar = jnp.broadcast_to(vab[0:1, lo:lo + slab], (R, slab))
        ai = jnp.broadcast_to(vab[1:2, lo:lo + slab], (R, slab))

        hr, hi = sre[:, lo:lo + slab], sim[:, lo:lo + slab]
        for t in range(T):
            rows = slice(t * R, (t + 1) * R)
            hr, hi = (ar * hr - ai * hi + bu[rows, lo:lo + slab],
                      ar * hi + ai * hr + bu[rows, NS + lo:NS + lo + slab])
            bu[rows, lo:lo + slab] = hr
            bu[rows, NS + lo:NS + lo + slab] = hi
        sre[:, lo:lo + slab] = hr
        sim[:, lo:lo + slab] = hi

    for k in range(NS // C_BLK):
        hre = bu[:, k * C_BLK:(k + 1) * C_BLK].astype(BF16)
        him = bu[:, NS + k * C_BLK:NS + (k + 1) * C_BLK].astype(BF16)
        sl = slice(k * 128, (k + 1) * 128)
        y = _dot(hre, cre[k]) + _dot(him, cim[k]) + vrow(ROW_SSD, sl) * us_buf[:, sl]
        us_buf[:, sl] = jax.nn.gelu(y)
    zc = us_buf[...]
    glu = zc * jax.nn.sigmoid(_dot(zc.astype(BF16), gluw[...]) + vrow(ROW_GLB))
    out_c = _dot(glu.astype(BF16), pc[...])
    m = m_acc[...] + jax.nn.sigmoid(proj(OFF_GL + 2 * D_MODEL, N_IN)) * out_c

    out = _dot(m.astype(BF16), wout[...])
    _store_rows(x1_ref, _layer_norm(ALPHA * _load_rows(x_ref) + out, vln[0:1, :], vln[1:2, :]))

    xa_ext[0:KA, :] = xa_ext[TR:TR + KA, :]
    u_ext[0:KB, :] = u_ext[TR:TR + KB, :]

    @pl.when(i == pl.num_programs(0) - 1)
    def _():
        for k in range(LRU_CONV - 1):
            ca_out[:, k * WIDTH:(k + 1) * WIDTH] = xa_ext[k * R:(k + 1) * R, :]
        for k in range(SC_CONV - 1):
            cb_out[:, k * WIDTH:(k + 1) * WIDTH] = u_ext[k * R:(k + 1) * R, :]
        ha_out[...] = ha[...]
        sre_out[...] = sre[...]
        sim_out[...] = sim[...]


def _layer_block(layer, shape):
    zeros = (0,) * (len(shape) - 1)
    return pl.BlockSpec((None,) + tuple(shape[1:]), lambda i: (layer,) + zeros, pipeline_mode=pl.Buffered(1))


def _mixer(x, layer, states, wts):
    R = x.shape[0]
    T = ROW_TILE // R
    n_steps = x.shape[1] // (T * D_MODEL)
    assert ROW_TILE % R == 0 and x.shape[1] % (T * D_MODEL) == 0
    x_spec = pl.BlockSpec((R, T * D_MODEL), lambda i: (0, i))
    state_shapes = [(R, (LRU_CONV - 1) * WIDTH), (R, WIDTH), (R, (SC_CONV - 1) * WIDTH), (R, NS), (R, NS)]
    states = () if states is None else tuple(states)
    in_specs = ([x_spec] + [_layer_block(layer, s.shape) for s in states]
                + [_layer_block(layer, w.shape) for w in wts])
    out_shape = [jax.ShapeDtypeStruct(x.shape, F32)] + [jax.ShapeDtypeStruct(s, F32) for s in state_shapes]
    out_specs = [x_spec] + [pl.BlockSpec(s, lambda i: (0, 0)) for s in state_shapes]
    scratch = [
        pltpu.VMEM(((LRU_CONV - 1) * R + ROW_TILE, WIDTH), F32),
        pltpu.VMEM(((SC_CONV - 1) * R + ROW_TILE, WIDTH), F32),
        pltpu.VMEM((ROW_TILE, WIDTH), F32),
        pltpu.VMEM((ROW_TILE, WIDTH), F32),
        pltpu.VMEM((ROW_TILE, WIDTH), F32),
        pltpu.VMEM((ROW_TILE, 2 * NS), F32),
        pltpu.VMEM((ROW_TILE, D_MODEL), F32),
        pltpu.VMEM((R, WIDTH), F32),
        pltpu.VMEM((R, NS), F32),
        pltpu.VMEM((R, NS), F32),
    ]
    outs = pl.pallas_call(
        functools.partial(_mixer_kernel, R, bool(states)),
        grid=(n_steps,), in_specs=in_specs, out_specs=out_specs, out_shape=out_shape,
        scratch_shapes=scratch,
        compiler_params=pltpu.CompilerParams(dimension_semantics=("arbitrary",),
                                             vmem_limit_bytes=V7X_VMEM_LIMIT_BYTES),
        name=f"mixer_r{R}",
    )(x, *states, *wts)
    return outs[0], outs[1:]


def _mlp_rows(x, up, down, vln):
    xb = x.astype(BF16)
    acc = ALPHA * x
    for c in range(D_FF // FF_BLK):
        h = _dot(xb, up[:, c * FF_BLK:(c + 1) * FF_BLK])
        h = jnp.square(jnp.maximum(h, 0.0)).astype(BF16)
        acc = acc + _dot(h, down[c * FF_BLK:(c + 1) * FF_BLK, :])
    return _layer_norm(acc, vln[0:1, :], vln[1:2, :])


def _mlp_kernel(n_prompt, xp_ref, xs_ref, up, down, vln, op_ref, os_ref):
    i = pl.program_id(0)

    @pl.when(i < n_prompt)
    def _():
        _store_rows(op_ref, _mlp_rows(_load_rows(xp_ref), up, down, vln))

    @pl.when(i == n_prompt)
    def _():
        _store_rows(os_ref, _mlp_rows(_load_rows(xs_ref), up, down, vln))


def _mlp(xp, xs, layer, up, down, vln):
    rp, rs = xp.shape[0], xs.shape[0]
    tp = ROW_TILE // rp
    n_prompt = xp.shape[1] // (tp * D_MODEL)
    assert xs.shape[1] * rs == ROW_TILE * D_MODEL
    p_spec = pl.BlockSpec((rp, tp * D_MODEL), lambda i: (0, jnp.minimum(i, n_prompt - 1)))
    s_spec = pl.BlockSpec(xs.shape, lambda i: (0, 0))
    return pl.pallas_call(
        functools.partial(_mlp_kernel, n_prompt),
        grid=(n_prompt + 1,),
        in_specs=[p_spec, s_spec] + [_layer_block(layer, w.shape) for w in (up, down, vln)],
        out_specs=[p_spec, s_spec],
        out_shape=[jax.ShapeDtypeStruct(xp.shape, F32), jax.ShapeDtypeStruct(xs.shape, F32)],
        compiler_params=pltpu.CompilerParams(dimension_semantics=("arbitrary",),
                                             vmem_limit_bytes=V7X_VMEM_LIMIT_BYTES),
        name="mlp",
    )(xp, xs, up, down, vln)


def _block_diag(blocks):
    n, a, b = blocks.shape[-3:]
    eye = jnp.eye(n, dtype=blocks.dtype)
    out = jnp.einsum("...nab,nm->...namb", blocks, eye)
    return out.reshape(*blocks.shape[:-3], n * a, n * b)


def kernel(x_prompt, x_sample, state_lru_conv, state_lru_h, state_sconv, state_ssm_re, state_ssm_im,
           w_in, conv_a_w, conv_a_b, gate_x_w, gate_x_b, gate_a_w, gate_a_b, lru_lambda, conv_b_w,
           ssm_a_re, ssm_a_im, ssm_log_dt, ssm_b_re, ssm_b_im, ssm_c_re, ssm_c_im, ssm_d, glu_w, glu_b,
           proj_a, proj_b, proj_c, w_out, ln1_g, ln1_b, mlp_up, mlp_down, ln2_g, ln2_b):
    batch_p, batch_s = x_prompt.shape[0], x_sample.shape[0]
    bf = lambda w: w.astype(BF16)

    vab, bbar_re, bbar_im = _s5_prep(ssm_a_re, ssm_a_im, ssm_log_dt, ssm_b_re, ssm_b_im)

    def b_blocks(bbar):
        bb = bbar.reshape(DEPTH, SSM_GROUP, WIDTH // B_BLK, B_BLK // SSM_GROUP, SSM_STATE)
        return _block_diag(jnp.transpose(bb, (0, 2, 3, 1, 4)))

    bblk = bf(jnp.concatenate([b_blocks(bbar_re), b_blocks(bbar_im)], axis=-1))

    def c_blocks(c):
        cc = c.reshape(DEPTH, NS // C_BLK, C_BLK // SSM_STATE, SSM_GROUP, SSM_STATE)
        return bf(_block_diag(jnp.swapaxes(cc, -1, -2)))

    cre, cim = c_blocks(ssm_c_re), c_blocks(-ssm_c_im)

    def gate_blocks(w):
        heads = GATE_BLK // LRU_HEAD_DIM
        return bf(_block_diag(w.reshape(DEPTH, LRU_HEADS // heads, heads, LRU_HEAD_DIM, LRU_HEAD_DIM)))

    gxw, gaw = gate_blocks(gate_x_w), gate_blocks(gate_a_w)

    row = lambda v: v[:, None, :]
    vw = jnp.concatenate([conv_a_w, row(conv_a_b), row(gate_x_b), row(gate_a_b), row(lru_lambda),
                          conv_b_w, row(ssm_d), row(glu_b), jnp.zeros((DEPTH, 3, WIDTH), F32)], axis=1)
    vln1 = jnp.stack([ln1_g, ln1_b], axis=1)
    vln2 = jnp.stack([ln2_g, ln2_b], axis=1)
    wts = (bf(w_in), gxw, gaw, bf(proj_a), bf(proj_b), bf(proj_c), bf(glu_w), bf(w_out), bblk, cre, cim,
           vw, vln1, vab)
    up, down = bf(mlp_up), bf(mlp_down)

    states_s = (state_lru_conv.reshape(DEPTH, batch_s, (LRU_CONV - 1) * WIDTH), state_lru_h,
                state_sconv.reshape(DEPTH, batch_s, (SC_CONV - 1) * WIDTH),
                state_ssm_re.reshape(DEPTH, batch_s, NS), state_ssm_im.reshape(DEPTH, batch_s, NS))

    yp = x_prompt.reshape(batch_p, -1)
    ys = x_sample.reshape(batch_s, -1)
    new_p, new_s = [], []
    for layer in range(DEPTH):
        yp, stp = _mixer(yp, layer, None, wts)
        ys, sts = _mixer(ys, layer, states_s, wts)
        yp, ys = _mlp(yp, ys, layer, up, down, vln2)
        new_p.append(stp)
        new_s.append(sts)

    def collect(new, batch):
        ca, ha, cb, sre, sim = (jnp.stack([st[j] for st in new], 0) for j in range(5))
        shape_ssm = (DEPTH, batch, SSM_GROUPS, SSM_STATE)
        return (ca.reshape(DEPTH, batch, LRU_CONV - 1, WIDTH), ha, cb.reshape(DEPTH, batch, SC_CONV - 1, WIDTH),
                sre.reshape(shape_ssm), sim.reshape(shape_ssm))

    return ((yp.reshape(x_prompt.shape), ys.reshape(x_sample.shape))
            + collect(new_p, batch_p) + collect(new_s, batch_s))
```

```python
import functools

import jax
import jax.numpy as jnp
from jax import lax
from jax.experimental import pallas as pl
from jax.experimental.pallas import tpu as pltpu

F32 = jnp.float32
BF16 = jnp.bfloat16

D_MODEL = 1024
DEPTH = 2
WIDTH = 512
LRU_HEADS = 8
LRU_HEAD_DIM = WIDTH // LRU_HEADS
LRU_CONV = 4
LRU_C = 8.0
SC_CONV = 3
SSM_GROUP = 16
SSM_GROUPS = WIDTH // SSM_GROUP
SSM_STATE = 64
NS = SSM_GROUPS * SSM_STATE
D_FF = 4 * D_MODEL
N_IN = 6 * WIDTH + 3 * D_MODEL
ALPHA = (2 * DEPTH) ** 0.25
LN_EPS = 1e-5

OFF_XA, OFF_YA, OFF_SB, OFF_SC, OFF_SH, OFF_US, OFF_GL = 0, 512, 1024, 1536, 2048, 2560, 3072

ROW_TILE = 512
MIXER_ROW_TILE = 512
GATE_BLK = 256
B_BLK = 128
C_BLK = 512
FF_BLK = 1024
TAIL_PARTS = 4
TIME_PARTS = 2

ROW_CAW, ROW_CAB, ROW_GXB, ROW_GAB, ROW_LAM, ROW_CBW, ROW_SSD, ROW_GLB = 0, 4, 5, 6, 7, 8, 11, 12

V7X_VMEM_LIMIT_BYTES = 58 * 1024 * 1024


def _dot(a, b):
    return jnp.dot(a, b, preferred_element_type=F32)


def _layer_norm(x, g, b):
    mu = jnp.mean(x, axis=-1, keepdims=True)
    xc = x - mu
    var = jnp.mean(xc * xc, axis=-1, keepdims=True)
    return xc * lax.rsqrt(var + LN_EPS) * g + b


def _load_rows(x_ref, t0=0, t1=None):
    t1 = x_ref.shape[1] // D_MODEL if t1 is None else t1
    return jnp.concatenate([x_ref[:, t * D_MODEL:(t + 1) * D_MODEL] for t in range(t0, t1)], axis=0)


def _store_rows(o_ref, rows, t0=0):
    r = o_ref.shape[0]
    for t in range(rows.shape[0] // r):
        o_ref[:, (t0 + t) * D_MODEL:(t0 + t + 1) * D_MODEL] = rows[t * r:(t + 1) * r, :]


def _s5_prep_kernel(a_re_ref, a_im_ref, log_dt_ref, b_re_ref, b_im_ref,
                    abar_ref, bbar_re_ref, bbar_im_ref):
    a_re = a_re_ref[...]
    a_im = a_im_ref[...]
    step = jnp.exp(log_dt_ref[...])
    mag = jnp.exp(step * a_re)
    abar_re = mag * jnp.cos(step * a_im)
    abar_im = mag * jnp.sin(step * a_im)
    den = a_re * a_re + a_im * a_im
    nr = abar_re - 1.0
    ni = abar_im
    coef_re = (nr * a_re + ni * a_im) / den
    coef_im = (ni * a_re - nr * a_im) / den
    b_re = b_re_ref[...]
    b_im = b_im_ref[...]
    abar_ref[:, 0:1, :] = abar_re
    abar_ref[:, 1:2, :] = abar_im
    bbar_re_ref[...] = coef_re * b_re - coef_im * b_im
    bbar_im_ref[...] = coef_re * b_im + coef_im * b_re


def _s5_prep(ssm_a_re, ssm_a_im, ssm_log_dt, ssm_b_re, ssm_b_im):
    flat = lambda a: a.reshape(DEPTH, 1, NS)
    chan_major = lambda b: jnp.transpose(b, (0, 3, 1, 2)).reshape(DEPTH, SSM_GROUP, NS)
    vec = jax.ShapeDtypeStruct((DEPTH, 2, NS), F32)
    mat = jax.ShapeDtypeStruct((DEPTH, SSM_GROUP, NS), F32)
    return pl.pallas_call(_s5_prep_kernel, out_shape=(vec, mat, mat), name="s5_prep")(
        flat(ssm_a_re), flat(ssm_a_im), flat(ssm_log_dt), chan_major(ssm_b_re), chan_major(ssm_b_im))


def _mixer_kernel(R, has_init, *refs):
    n_state = 5 if has_init else 0
    x_ref = refs[0]
    states_in = refs[1:1 + n_state]
    (w_in, gxw, gaw, pa, pb, pc, gluw, wout, bblk, cre, cim, vw, vln, vab) = refs[1 + n_state:15 + n_state]
    x1_ref, ca_out, ha_out, cb_out, sre_out, sim_out = refs[15 + n_state:21 + n_state]
    xa_ext, u_ext, us_buf, bu, ha, sre, sim = refs[21 + n_state:]
    T = MIXER_ROW_TILE // R
    TR = MIXER_ROW_TILE
    KA = (LRU_CONV - 1) * R
    KB = (SC_CONV - 1) * R
    i = pl.program_id(0)

    @pl.when(i == 0)
    def _():
        if has_init:
            ca_in, ha_in, cb_in, sre_in, sim_in = states_in
            for k in range(LRU_CONV - 1):
                xa_ext[k * R:(k + 1) * R, :] = ca_in[:, k * WIDTH:(k + 1) * WIDTH]
            for k in range(SC_CONV - 1):
                u_ext[k * R:(k + 1) * R, :] = cb_in[:, k * WIDTH:(k + 1) * WIDTH]
            ha[...] = ha_in[...]
            sre[...] = sre_in[...]
            sim[...] = sim_in[...]
        else:
            xa_ext[0:KA, :] = jnp.zeros((KA, WIDTH), F32)
            u_ext[0:KB, :] = jnp.zeros((KB, WIDTH), F32)
            ha[...] = jnp.zeros((R, WIDTH), F32)
            sre[...] = jnp.zeros((R, NS), F32)
            sim[...] = jnp.zeros((R, NS), F32)

    xb = _load_rows(x_ref).astype(BF16)

    def proj(lo, hi):
        return _dot(xb, w_in[:, lo:hi])

    def vrow(r, sl=slice(None)):
        return vw[r:r + 1, sl]

    parts = [(p * T // TIME_PARTS, (p + 1) * T // TIME_PARTS) for p in range(TIME_PARTS)]

    def taps(ext, w_row, n_taps, t0, t1):
        acc = ext[t0 * R:t1 * R, :] * vrow(w_row)
        for k in range(1, n_taps):
            acc = acc + ext[(t0 + k) * R:(t1 + k) * R, :] * vrow(w_row + k)
        return acc

    us_buf[...] = proj(OFF_US, OFF_GL)
    za = proj(OFF_XA, OFF_SB)
    xa_ext[KA:KA + TR, :] = za[:, :WIDTH]
    for m in range(WIDTH // B_BLK):
        usb = us_buf[:, m * B_BLK:(m + 1) * B_BLK].astype(BF16)
        bum = _dot(usb, bblk[m])
        bu[:, m * C_BLK:(m + 1) * C_BLK] = bum[:, :C_BLK]
        bu[:, NS + m * C_BLK:NS + (m + 1) * C_BLK] = bum[:, C_BLK:]
    zb = proj(OFF_SB, OFF_US)
    u_ext[KB:KB + TR, :] = zb[:, WIDTH:2 * WIDTH] * zb[:, 2 * WIDTH:]

    softplus_neg_lam = jax.nn.softplus(-vrow(ROW_LAM))
    gate_in = []
    for t0, t1 in parts:
        xa_c = taps(xa_ext, ROW_CAW, LRU_CONV, t0, t1) + vrow(ROW_CAB)
        pre = []
        for hh in range(WIDTH // GATE_BLK):
            xcb = xa_c[:, hh * GATE_BLK:(hh + 1) * GATE_BLK].astype(BF16)
            pre.append((_dot(xcb, gxw[hh]), _dot(xcb, gaw[hh])))
        gate_in.append((xa_c, pre))

    h_lru = ha[...]
    lhs_a = []
    for (t0, t1), (xa_c, pre) in zip(parts, gate_in):
        a_in, b_in = [], []
        for hh in range(WIDTH // GATE_BLK):
            sl = slice(hh * GATE_BLK, (hh + 1) * GATE_BLK)
            gx = jax.nn.sigmoid(pre[hh][0] + vrow(ROW_GXB, sl))
            ga = jax.nn.sigmoid(pre[hh][1] + vrow(ROW_GAB, sl))
            a = jnp.exp(-LRU_C * ga * softplus_neg_lam[:, sl])
            a_in.append(a)
            b_in.append(jnp.sqrt(1.0 - a * a) * (gx * xa_c[:, sl]))
        a_in, b_in = jnp.concatenate(a_in, axis=1), jnp.concatenate(b_in, axis=1)
        h_rows = []
        for t in range(t1 - t0):
            h_lru = a_in[t * R:(t + 1) * R, :] * h_lru + b_in[t * R:(t + 1) * R, :]
            h_rows.append(h_lru)
        h_a = jnp.concatenate(h_rows, axis=0)
        lhs_a.append((h_a * jax.nn.gelu(za[t0 * R:t1 * R, WIDTH:])).astype(BF16))
    ha[...] = h_lru

    lhs_b = [(zb[t0 * R:t1 * R, :WIDTH] * taps(u_ext, ROW_CBW, SC_CONV, t0, t1)).astype(BF16) for t0, t1 in parts]

    slab = max(128, min(512, 4096 // R))
    slabs = [(slice(lo, lo + slab), slice(NS + lo, NS + lo + slab)) for lo in range(0, NS, slab)]
    abar = [(jnp.broadcast_to(vab[0:1, sl], (R, slab)), jnp.broadcast_to(vab[1:2, sl], (R, slab))) for sl, _ in slabs]
    h_s5 = [(sre[:, sl], sim[:, sl]) for sl, _ in slabs]
    for t0, t1 in parts:
        for s, (sl, sl_im) in enumerate(slabs):
            (ar, ai), (hr, hi) = abar[s], h_s5[s]
            for t in range(t0, t1):
                step = slice(t * R, (t + 1) * R)
                hr, hi = ar * hr - ai * hi + bu[step, sl], ar * hi + ai * hr + bu[step, sl_im]
                bu[step, sl] = hr
                bu[step, sl_im] = hi
            h_s5[s] = (hr, hi)
    for (sl, _), (hr, hi) in zip(slabs, h_s5):
        sre[:, sl] = hr
        sim[:, sl] = hi

    gates, out_ab, zc, out_c = {}, {}, {}, {}

    def stage_gates(p):
        t0, t1 = parts[p]
        xb_p = xb[t0 * R:t1 * R, :]
        gates[p] = [jax.nn.sigmoid(_dot(xb_p, w_in[:, OFF_GL + j * D_MODEL:OFF_GL + (j + 1) * D_MODEL]))
                    for j in range(3)]

    def stage_branches(p):
        t0, t1 = parts[p]
        rows = slice(t0 * R, t1 * R)
        out_ab[p] = (_dot(lhs_a[p], pa[...]), _dot(lhs_b[p], pb[...]))
        y = []
        for k in range(NS // C_BLK):
            hre = bu[rows, k * C_BLK:(k + 1) * C_BLK].astype(BF16)
            him = bu[rows, NS + k * C_BLK:NS + (k + 1) * C_BLK].astype(BF16)
            sl = slice(k * 128, (k + 1) * 128)
            y.append(_dot(hre, cre[k]) + _dot(him, cim[k]) + vrow(ROW_SSD, sl) * us_buf[rows, sl])
        zc[p] = jax.nn.gelu(jnp.concatenate(y, axis=1))

    def stage_glu(p):
        zc[p] = (zc[p] * jax.nn.sigmoid(_dot(zc[p].astype(BF16), gluw[...]) + vrow(ROW_GLB))).astype(BF16)

    def stage_proj_c(p):
        out_c[p] = _dot(zc[p], pc[...])

    def stage_out(p):
        t0, t1 = parts[p]
        m = gates[p][0] * out_ab[p][0] + gates[p][1] * out_ab[p][1] + gates[p][2] * out_c[p]
        out = _dot(m.astype(BF16), wout[...])
        y1 = _layer_norm(ALPHA * _load_rows(x_ref, t0, t1) + out, vln[0:1, :], vln[1:2, :])
        _store_rows(x1_ref, y1, t0)

    for stage in (stage_gates, stage_branches, stage_glu, stage_proj_c, stage_out):
        for p in range(TIME_PARTS):
            stage(p)

    xa_ext[0:KA, :] = xa_ext[TR:TR + KA, :]
    u_ext[0:KB, :] = u_ext[TR:TR + KB, :]

    @pl.when(i == pl.num_programs(0) - 1)
    def _():
        for k in range(LRU_CONV - 1):
            ca_out[:, k * WIDTH:(k + 1) * WIDTH] = xa_ext[k * R:(k + 1) * R, :]
        for k in range(SC_CONV - 1):
            cb_out[:, k * WIDTH:(k + 1) * WIDTH] = u_ext[k * R:(k + 1) * R, :]
        ha_out[...] = ha[...]
        sre_out[...] = sre[...]
        sim_out[...] = sim[...]


def _layer_block(layer, shape):
    zeros = (0,) * (len(shape) - 1)
    return pl.BlockSpec((None,) + tuple(shape[1:]), lambda i: (layer,) + zeros, pipeline_mode=pl.Buffered(1))


def _mixer(x, layer, states, wts):
    R = x.shape[0]
    tile = MIXER_ROW_TILE
    T = tile // R
    n_steps = x.shape[1] // (T * D_MODEL)
    assert tile % R == 0 and x.shape[1] % (T * D_MODEL) == 0
    x_spec = pl.BlockSpec((R, T * D_MODEL), lambda i: (0, i))
    state_shapes = [(R, (LRU_CONV - 1) * WIDTH), (R, WIDTH), (R, (SC_CONV - 1) * WIDTH), (R, NS), (R, NS)]
    states = () if states is None else tuple(states)
    in_specs = ([x_spec] + [_layer_block(layer, s.shape) for s in states]
                + [_layer_block(layer, w.shape) for w in wts])
    out_shape = [jax.ShapeDtypeStruct(x.shape, F32)] + [jax.ShapeDtypeStruct(s, F32) for s in state_shapes]
    out_specs = [x_spec] + [pl.BlockSpec(s, lambda i: (0, 0)) for s in state_shapes]
    scratch = [
        pltpu.VMEM(((LRU_CONV - 1) * R + tile, WIDTH), F32),
        pltpu.VMEM(((SC_CONV - 1) * R + tile, WIDTH), F32),
        pltpu.VMEM((tile, WIDTH), F32),
        pltpu.VMEM((tile, 2 * NS), F32),
        pltpu.VMEM((R, WIDTH), F32),
        pltpu.VMEM((R, NS), F32),
        pltpu.VMEM((R, NS), F32),
    ]
    outs = pl.pallas_call(
        functools.partial(_mixer_kernel, R, bool(states)),
        grid=(n_steps,), in_specs=in_specs, out_specs=out_specs, out_shape=out_shape,
        scratch_shapes=scratch,
        compiler_params=pltpu.CompilerParams(dimension_semantics=("arbitrary",),
                                             vmem_limit_bytes=V7X_VMEM_LIMIT_BYTES),
        name=f"mixer_r{R}",
    )(x, *states, *wts)
    return outs[0], outs[1:]


def _mlp_block(x_ref, o_ref, up, down, vln):
    x = _load_rows(x_ref)
    xb = x.astype(BF16)
    acc = ALPHA * x
    n_ff = D_FF // FF_BLK
    for c in range(n_ff):
        h = _dot(xb, up[:, c * FF_BLK:(c + 1) * FF_BLK])
        h = jnp.square(jnp.maximum(h, 0.0)).astype(BF16)
        if c < n_ff - 1:
            acc = acc + _dot(h, down[c * FF_BLK:(c + 1) * FF_BLK, :])
    r = x_ref.shape[0]
    steps = x.shape[0] // r
    for part in range(TAIL_PARTS):
        t0, t1 = part * steps // TAIL_PARTS, (part + 1) * steps // TAIL_PARTS
        rows = slice(t0 * r, t1 * r)
        f = acc[rows, :] + _dot(h[rows, :], down[(n_ff - 1) * FF_BLK:, :])
        _store_rows(o_ref, _layer_norm(f, vln[0:1, :], vln[1:2, :]), t0)


def _mlp_kernel(n_prompt, xp_ref, xs_ref, up, down, vln, op_ref, os_ref):
    i = pl.program_id(0)

    @pl.when(i < n_prompt)
    def _():
        _mlp_block(xp_ref, op_ref, up, down, vln)

    @pl.when(i == n_prompt)
    def _():
        _mlp_block(xs_ref, os_ref, up, down, vln)


def _mlp(xp, xs, layer, up, down, vln):
    rp, rs = xp.shape[0], xs.shape[0]
    tp = ROW_TILE // rp
    n_prompt = xp.shape[1] // (tp * D_MODEL)
    assert xs.shape[1] * rs == ROW_TILE * D_MODEL
    p_spec = pl.BlockSpec((rp, tp * D_MODEL), lambda i: (0, jnp.minimum(i, n_prompt - 1)))
    s_spec = pl.BlockSpec(xs.shape, lambda i: (0, 0))
    return pl.pallas_call(
        functools.partial(_mlp_kernel, n_prompt),
        grid=(n_prompt + 1,),
        in_specs=[p_spec, s_spec] + [_layer_block(layer, w.shape) for w in (up, down, vln)],
        out_specs=[p_spec, s_spec],
        out_shape=[jax.ShapeDtypeStruct(xp.shape, F32), jax.ShapeDtypeStruct(xs.shape, F32)],
        compiler_params=pltpu.CompilerParams(dimension_semantics=("arbitrary",),
                                             vmem_limit_bytes=V7X_VMEM_LIMIT_BYTES),
        name="mlp",
    )(xp, xs, up, down, vln)


def _block_diag(blocks):
    n, a, b = blocks.shape[-3:]
    eye = jnp.eye(n, dtype=blocks.dtype)
    out = jnp.einsum("...nab,nm->...namb", blocks, eye)
    return out.reshape(*blocks.shape[:-3], n * a, n * b)


def kernel(x_prompt, x_sample, state_lru_conv, state_lru_h, state_sconv, state_ssm_re, state_ssm_im,
           w_in, conv_a_w, conv_a_b, gate_x_w, gate_x_b, gate_a_w, gate_a_b, lru_lambda, conv_b_w,
           ssm_a_re, ssm_a_im, ssm_log_dt, ssm_b_re, ssm_b_im, ssm_c_re, ssm_c_im, ssm_d, glu_w, glu_b,
           proj_a, proj_b, proj_c, w_out, ln1_g, ln1_b, mlp_up, mlp_down, ln2_g, ln2_b):
    batch_p, batch_s = x_prompt.shape[0], x_sample.shape[0]
    bf = lambda w: w.astype(BF16)

    vab, bbar_re, bbar_im = _s5_prep(ssm_a_re, ssm_a_im, ssm_log_dt, ssm_b_re, ssm_b_im)

    def b_blocks(bbar):
        bb = bbar.reshape(DEPTH, SSM_GROUP, WIDTH // B_BLK, B_BLK // SSM_GROUP, SSM_STATE)
        return _block_diag(jnp.transpose(bb, (0, 2, 3, 1, 4)))

    bblk = bf(jnp.concatenate([b_blocks(bbar_re), b_blocks(bbar_im)], axis=-1))

    def c_blocks(c):
        cc = c.reshape(DEPTH, NS // C_BLK, C_BLK // SSM_STATE, SSM_GROUP, SSM_STATE)
        return bf(_block_diag(jnp.swapaxes(cc, -1, -2)))

    cre, cim = c_blocks(ssm_c_re), c_blocks(-ssm_c_im)

    def gate_blocks(w):
        heads = GATE_BLK // LRU_HEAD_DIM
        return bf(_block_diag(w.reshape(DEPTH, LRU_HEADS // heads, heads, LRU_HEAD_DIM, LRU_HEAD_DIM)))

    gxw, gaw = gate_blocks(gate_x_w), gate_blocks(gate_a_w)

    row = lambda v: v[:, None, :]
    vw = jnp.concatenate([conv_a_w, row(conv_a_b), row(gate_x_b), row(gate_a_b), row(lru_lambda),
                          conv_b_w, row(ssm_d), row(glu_b), jnp.zeros((DEPTH, 3, WIDTH), F32)], axis=1)
    vln1 = jnp.stack([ln1_g, ln1_b], axis=1)
    vln2 = jnp.stack([ln2_g, ln2_b], axis=1)
    wts = (bf(w_in), gxw, gaw, bf(proj_a), bf(proj_b), bf(proj_c), bf(glu_w), bf(w_out), bblk, cre, cim,
           vw, vln1, vab)
    up, down = bf(mlp_up), bf(mlp_down)

    states_s = (state_lru_conv.reshape(DEPTH, batch_s, (LRU_CONV - 1) * WIDTH), state_lru_h,
                state_sconv.reshape(DEPTH, batch_s, (SC_CONV - 1) * WIDTH),
                state_ssm_re.reshape(DEPTH, batch_s, NS), state_ssm_im.reshape(DEPTH, batch_s, NS))

    yp = x_prompt.reshape(batch_p, -1)
    ys = x_sample.reshape(batch_s, -1)
    new_p, new_s = [], []
    for layer in range(DEPTH):
        yp, stp = _mixer(yp, layer, None, wts)
        ys, sts = _mixer(ys, layer, states_s, wts)
        yp, ys = _mlp(yp, ys, layer, up, down, vln2)
        new_p.append(stp)
        new_s.append(sts)

    def collect(new, batch):
        ca, ha, cb, sre, sim = (jnp.stack([st[j] for st in new], 0) for j in range(5))
        shape_ssm = (DEPTH, batch, SSM_GROUPS, SSM_STATE)
        return (ca.reshape(DEPTH, batch, LRU_CONV - 1, WIDTH), ha, cb.reshape(DEPTH, batch, SC_CONV - 1, WIDTH),
                sre.reshape(shape_ssm), sim.reshape(shape_ssm))

    return ((yp.reshape(x_prompt.shape), ys.reshape(x_sample.shape))
            + collect(new_p, batch_p) + collect(new_s, batch_s))
```

```python
import functools

import jax
import jax.numpy as jnp
from jax import lax
from jax.experimental import pallas as pl
from jax.experimental.pallas import tpu as pltpu

F32 = jnp.float32
BF16 = jnp.bfloat16

D_MODEL = 1024
DEPTH = 2
WIDTH = 512
LRU_HEADS = 8
LRU_HEAD_DIM = WIDTH // LRU_HEADS
LRU_CONV = 4
LRU_C = 8.0
SC_CONV = 3
SSM_GROUP = 16
SSM_GROUPS = WIDTH // SSM_GROUP
SSM_STATE = 64
NS = SSM_GROUPS * SSM_STATE
D_FF = 4 * D_MODEL
N_IN = 6 * WIDTH + 3 * D_MODEL
ALPHA = (2 * DEPTH) ** 0.25
LN_EPS = 1e-5

OFF_XA, OFF_YA, OFF_SB, OFF_SC, OFF_SH, OFF_US, OFF_GL = 0, 512, 1024, 1536, 2048, 2560, 3072

ROW_TILE = 512
MIXER_ROW_TILE = 512
GATE_BLK = 256
B_BLK = 128
C_BLK = 512
FF_BLK = 1024
TAIL_PARTS = 4
TIME_PARTS = 2

ROW_CAW, ROW_CAB, ROW_GXB, ROW_GAB, ROW_LAM, ROW_CBW, ROW_SSD, ROW_GLB = 0, 4, 5, 6, 7, 8, 11, 12

V7X_VMEM_LIMIT_BYTES = 58 * 1024 * 1024


def _dot(a, b):
    return jnp.dot(a, b, preferred_element_type=F32)


def _layer_norm(x, g, b):
    mu = jnp.mean(x, axis=-1, keepdims=True)
    xc = x - mu
    var = jnp.mean(xc * xc, axis=-1, keepdims=True)
    return xc * lax.rsqrt(var + LN_EPS) * g + b


def _load_rows(x_ref, t0=0, t1=None):
    t1 = x_ref.shape[1] // D_MODEL if t1 is None else t1
    return jnp.concatenate([x_ref[:, t * D_MODEL:(t + 1) * D_MODEL] for t in range(t0, t1)], axis=0)


def _store_rows(o_ref, rows, t0=0):
    r = o_ref.shape[0]
    for t in range(rows.shape[0] // r):
        o_ref[:, (t0 + t) * D_MODEL:(t0 + t + 1) * D_MODEL] = rows[t * r:(t + 1) * r, :]


def _s5_prep_kernel(a_re_ref, a_im_ref, log_dt_ref, b_re_ref, b_im_ref,
                    abar_ref, bbar_re_ref, bbar_im_ref):
    a_re = a_re_ref[...]
    a_im = a_im_ref[...]
    step = jnp.exp(log_dt_ref[...])
    mag = jnp.exp(step * a_re)
    abar_re = mag * jnp.cos(step * a_im)
    abar_im = mag * jnp.sin(step * a_im)
    den = a_re * a_re + a_im * a_im
    nr = abar_re - 1.0
    ni = abar_im
    coef_re = (nr * a_re + ni * a_im) / den
    coef_im = (ni * a_re - nr * a_im) / den
    b_re = b_re_ref[...]
    b_im = b_im_ref[...]
    abar_ref[:, 0:1, :] = abar_re
    abar_ref[:, 1:2, :] = abar_im
    bbar_re_ref[...] = coef_re * b_re - coef_im * b_im
    bbar_im_ref[...] = coef_re * b_im + coef_im * b_re


def _s5_prep(ssm_a_re, ssm_a_im, ssm_log_dt, ssm_b_re, ssm_b_im):
    flat = lambda a: a.reshape(DEPTH, 1, NS)
    chan_major = lambda b: jnp.transpose(b, (0, 3, 1, 2)).reshape(DEPTH, SSM_GROUP, NS)
    vec = jax.ShapeDtypeStruct((DEPTH, 2, NS), F32)
    mat = jax.ShapeDtypeStruct((DEPTH, SSM_GROUP, NS), F32)
    return pl.pallas_call(_s5_prep_kernel, out_shape=(vec, mat, mat), name="s5_prep")(
        flat(ssm_a_re), flat(ssm_a_im), flat(ssm_log_dt), chan_major(ssm_b_re), chan_major(ssm_b_im))


def _mixer_kernel(R, has_init, *refs):
    n_state = 5 if has_init else 0
    x_ref = refs[0]
    states_in = refs[1:1 + n_state]
    (w_in, gxw, gaw, pa, pb, pc, gluw, wout, bblk, cre, cim, vw, vln, vab) = refs[1 + n_state:15 + n_state]
    x1_ref, ca_out, ha_out, cb_out, sre_out, sim_out = refs[15 + n_state:21 + n_state]
    xa_ext, u_ext, us_buf, bu, ha, sre, sim = refs[21 + n_state:]
    T = MIXER_ROW_TILE // R
    TR = MIXER_ROW_TILE
    KA = (LRU_CONV - 1) * R
    KB = (SC_CONV - 1) * R
    i = pl.program_id(0)

    @pl.when(i == 0)
    def _():
        if has_init:
            ca_in, ha_in, cb_in, sre_in, sim_in = states_in
            for k in range(LRU_CONV - 1):
                xa_ext[k * R:(k + 1) * R, :] = ca_in[:, k * WIDTH:(k + 1) * WIDTH]
            for k in range(SC_CONV - 1):
                u_ext[k * R:(k + 1) * R, :] = cb_in[:, k * WIDTH:(k + 1) * WIDTH]
            ha[...] = ha_in[...]
            sre[...] = sre_in[...]
            sim[...] = sim_in[...]
        else:
            xa_ext[0:KA, :] = jnp.zeros((KA, WIDTH), F32)
            u_ext[0:KB, :] = jnp.zeros((KB, WIDTH), F32)
            ha[...] = jnp.zeros((R, WIDTH), F32)
            sre[...] = jnp.zeros((R, NS), F32)
            sim[...] = jnp.zeros((R, NS), F32)

    xb = _load_rows(x_ref).astype(BF16)

    def proj(lo, hi):
        return _dot(xb, w_in[:, lo:hi])

    def vrow(r, sl=slice(None)):
        return vw[r:r + 1, sl]

    parts = [(p * T // TIME_PARTS, (p + 1) * T // TIME_PARTS) for p in range(TIME_PARTS)]

    def taps(ext, w_row, n_taps, t0, t1):
        acc = ext[t0 * R:t1 * R, :] * vrow(w_row)
        for k in range(1, n_taps):
            acc = acc + ext[(t0 + k) * R:(t1 + k) * R, :] * vrow(w_row + k)
        return acc

    us_buf[...] = proj(OFF_US, OFF_GL)
    za = []
    for m in range(WIDTH // B_BLK):
        usb = us_buf[:, m * B_BLK:(m + 1) * B_BLK].astype(BF16)
        bum = _dot(usb, bblk[m])
        bu[:, m * C_BLK:(m + 1) * C_BLK] = bum[:, :C_BLK]
        bu[:, NS + m * C_BLK:NS + (m + 1) * C_BLK] = bum[:, C_BLK:]
        za.append(proj(OFF_XA + m * 2 * B_BLK, OFF_XA + (m + 1) * 2 * B_BLK))
    za = jnp.concatenate(za, axis=1)
    xa_ext[KA:KA + TR, :] = za[:, :WIDTH]
    zb = proj(OFF_SB, OFF_US)
    u_ext[KB:KB + TR, :] = zb[:, WIDTH:2 * WIDTH] * zb[:, 2 * WIDTH:]

    softplus_neg_lam = jax.nn.softplus(-vrow(ROW_LAM))
    gate_in = []
    for t0, t1 in parts:
        xa_c = taps(xa_ext, ROW_CAW, LRU_CONV, t0, t1) + vrow(ROW_CAB)
        pre = []
        for hh in range(WIDTH // GATE_BLK):
            xcb = xa_c[:, hh * GATE_BLK:(hh + 1) * GATE_BLK].astype(BF16)
            pre.append((_dot(xcb, gxw[hh]), _dot(xcb, gaw[hh])))
        gate_in.append((xa_c, pre))

    h_lru = ha[...]
    lhs_a = []
    for (t0, t1), (xa_c, pre) in zip(parts, gate_in):
        a_in, b_in = [], []
        for hh in range(WIDTH // GATE_BLK):
            sl = slice(hh * GATE_BLK, (hh + 1) * GATE_BLK)
            gx = jax.nn.sigmoid(pre[hh][0] + vrow(ROW_GXB, sl))
            ga = jax.nn.sigmoid(pre[hh][1] + vrow(ROW_GAB, sl))
            a = jnp.exp(-LRU_C * ga * softplus_neg_lam[:, sl])
            a_in.append(a)
            b_in.append(jnp.sqrt(1.0 - a * a) * (gx * xa_c[:, sl]))
        a_in, b_in = jnp.concatenate(a_in, axis=1), jnp.concatenate(b_in, axis=1)
        h_rows = []
        for t in range(t1 - t0):
            h_lru = a_in[t * R:(t + 1) * R, :] * h_lru + b_in[t * R:(t + 1) * R, :]
            h_rows.append(h_lru)
        h_a = jnp.concatenate(h_rows, axis=0)
        lhs_a.append((h_a * jax.nn.gelu(za[t0 * R:t1 * R, WIDTH:])).astype(BF16))
    ha[...] = h_lru

    lhs_b = [(zb[t0 * R:t1 * R, :WIDTH] * taps(u_ext, ROW_CBW, SC_CONV, t0, t1)).astype(BF16) for t0, t1 in parts]

    slab = max(128, min(512, 4096 // R))
    slabs = [(slice(lo, lo + slab), slice(NS + lo, NS + lo + slab)) for lo in range(0, NS, slab)]
    abar = [(jnp.broadcast_to(vab[0:1, sl], (R, slab)), jnp.broadcast_to(vab[1:2, sl], (R, slab))) for sl, _ in slabs]
    h_s5 = [(sre[:, sl], sim[:, sl]) for sl, _ in slabs]
    for t0, t1 in parts:
        for s, (sl, sl_im) in enumerate(slabs):
            (ar, ai), (hr, hi) = abar[s], h_s5[s]
            for t in range(t0, t1):
                step = slice(t * R, (t + 1) * R)
                hr, hi = ar * hr - ai * hi + bu[step, sl], ar * hi + ai * hr + bu[step, sl_im]
                bu[step, sl] = hr
                bu[step, sl_im] = hi
            h_s5[s] = (hr, hi)
    for (sl, _), (hr, hi) in zip(slabs, h_s5):
        sre[:, sl] = hr
        sim[:, sl] = hi

    gates, out_ab, zc, out_c = {}, {}, {}, {}

    def stage_gates(p):
        t0, t1 = parts[p]
        xb_p = xb[t0 * R:t1 * R, :]
        gates[p] = [jax.nn.sigmoid(_dot(xb_p, w_in[:, OFF_GL + j * D_MODEL:OFF_GL + (j + 1) * D_MODEL]))
                    for j in range(3)]

    def stage_branches(p):
        t0, t1 = parts[p]
        rows = slice(t0 * R, t1 * R)
        out_ab[p] = (_dot(lhs_a[p], pa[...]), _dot(lhs_b[p], pb[...]))
        y = []
        for k in range(NS // C_BLK):
            hre = bu[rows, k * C_BLK:(k + 1) * C_BLK].astype(BF16)
            him = bu[rows, NS + k * C_BLK:NS + (k + 1) * C_BLK].astype(BF16)
            sl = slice(k * 128, (k + 1) * 128)
            y.append(_dot(hre, cre[k]) + _dot(him, cim[k]) + vrow(ROW_SSD, sl) * us_buf[rows, sl])
        zc[p] = jax.nn.gelu(jnp.concatenate(y, axis=1))

    def stage_glu(p):
        zc[p] = (zc[p] * jax.nn.sigmoid(_dot(zc[p].astype(BF16), gluw[...]) + vrow(ROW_GLB))).astype(BF16)

    def stage_proj_c(p):
        out_c[p] = _dot(zc[p], pc[...])

    def stage_out(p):
        t0, t1 = parts[p]
        m = gates[p][0] * out_ab[p][0] + gates[p][1] * out_ab[p][1] + gates[p][2] * out_c[p]
        out = _dot(m.astype(BF16), wout[...])
        y1 = _layer_norm(ALPHA * _load_rows(x_ref, t0, t1) + out, vln[0:1, :], vln[1:2, :])
        _store_rows(x1_ref, y1, t0)

    for stage in (stage_gates, stage_branches, stage_glu, stage_proj_c, stage_out):
        for p in range(TIME_PARTS):
            stage(p)

    xa_ext[0:KA, :] = xa_ext[TR:TR + KA, :]
    u_ext[0:KB, :] = u_ext[TR:TR + KB, :]

    @pl.when(i == pl.num_programs(0) - 1)
    def _():
        for k in range(LRU_CONV - 1):
            ca_out[:, k * WIDTH:(k + 1) * WIDTH] = xa_ext[k * R:(k + 1) * R, :]
        for k in range(SC_CONV - 1):
            cb_out[:, k * WIDTH:(k + 1) * WIDTH] = u_ext[k * R:(k + 1) * R, :]
        ha_out[...] = ha[...]
        sre_out[...] = sre[...]
        sim_out[...] = sim[...]


def _layer_block(layer, shape):
    zeros = (0,) * (len(shape) - 1)
    return pl.BlockSpec((None,) + tuple(shape[1:]), lambda i: (layer,) + zeros, pipeline_mode=pl.Buffered(1))


def _mixer(x, layer, states, wts):
    R = x.shape[0]
    tile = MIXER_ROW_TILE
    T = tile // R
    n_steps = x.shape[1] // (T * D_MODEL)
    assert tile % R == 0 and x.shape[1] % (T * D_MODEL) == 0
    x_spec = pl.BlockSpec((R, T * D_MODEL), lambda i: (0, i))
    state_shapes = [(R, (LRU_CONV - 1) * WIDTH), (R, WIDTH), (R, (SC_CONV - 1) * WIDTH), (R, NS), (R, NS)]
    states = () if states is None else tuple(states)
    in_specs = ([x_spec] + [_layer_block(layer, s.shape) for s in states]
                + [_layer_block(layer, w.shape) for w in wts])
    out_shape = [jax.ShapeDtypeStruct(x.shape, F32)] + [jax.ShapeDtypeStruct(s, F32) for s in state_shapes]
    out_specs = [x_spec] + [pl.BlockSpec(s, lambda i: (0, 0)) for s in state_shapes]
    scratch = [
        pltpu.VMEM(((LRU_CONV - 1) * R + tile, WIDTH), F32),
        pltpu.VMEM(((SC_CONV - 1) * R + tile, WIDTH), F32),
        pltpu.VMEM((tile, WIDTH), F32),
        pltpu.VMEM((tile, 2 * NS), F32),
        pltpu.VMEM((R, WIDTH), F32),
        pltpu.VMEM((R, NS), F32),
        pltpu.VMEM((R, NS), F32),
    ]
    outs = pl.pallas_call(
        functools.partial(_mixer_kernel, R, bool(states)),
        grid=(n_steps,), in_specs=in_specs, out_specs=out_specs, out_shape=out_shape,
        scratch_shapes=scratch,
        compiler_params=pltpu.CompilerParams(dimension_semantics=("arbitrary",),
                                             vmem_limit_bytes=V7X_VMEM_LIMIT_BYTES),
        name=f"mixer_r{R}",
    )(x, *states, *wts)
    return outs[0], outs[1:]


def _mlp_block(x_ref, o_ref, up, down, vln):
    x = _load_rows(x_ref)
    xb = x.astype(BF16)
    acc = ALPHA * x
    n_ff = D_FF // FF_BLK
    for c in range(n_ff):
        h = _dot(xb, up[:, c * FF_BLK:(c + 1) * FF_BLK].astype(BF16))
        h = jnp.square(jnp.maximum(h, 0.0)).astype(BF16)
        if c < n_ff - 1:
            acc = acc + _dot(h, down[c * FF_BLK:(c + 1) * FF_BLK, :].astype(BF16))
    r = x_ref.shape[0]
    steps = x.shape[0] // r
    down_last = down[(n_ff - 1) * FF_BLK:, :].astype(BF16)
    for part in range(TAIL_PARTS):
        t0, t1 = part * steps // TAIL_PARTS, (part + 1) * steps // TAIL_PARTS
        rows = slice(t0 * r, t1 * r)
        f = acc[rows, :] + _dot(h[rows, :], down_last)
        _store_rows(o_ref, _layer_norm(f, vln[0:1, :], vln[1:2, :]), t0)


def _mlp_kernel(n_prompt, xp_ref, xs_ref, up, down, vln, op_ref, os_ref):
    i = pl.program_id(0)

    @pl.when(i < n_prompt)
    def _():
        _mlp_block(xp_ref, op_ref, up, down, vln)

    @pl.when(i == n_prompt)
    def _():
        _mlp_block(xs_ref, os_ref, up, down, vln)


def _mlp(xp, xs, layer, up, down, vln):
    rp, rs = xp.shape[0], xs.shape[0]
    tp = ROW_TILE // rp
    n_prompt = xp.shape[1] // (tp * D_MODEL)
    assert xs.shape[1] * rs == ROW_TILE * D_MODEL
    p_spec = pl.BlockSpec((rp, tp * D_MODEL), lambda i: (0, jnp.minimum(i, n_prompt - 1)))
    s_spec = pl.BlockSpec(xs.shape, lambda i: (0, 0))
    return pl.pallas_call(
        functools.partial(_mlp_kernel, n_prompt),
        grid=(n_prompt + 1,),
        in_specs=[p_spec, s_spec] + [_layer_block(layer, w.shape) for w in (up, down, vln)],
        out_specs=[p_spec, s_spec],
        out_shape=[jax.ShapeDtypeStruct(xp.shape, F32), jax.ShapeDtypeStruct(xs.shape, F32)],
        compiler_params=pltpu.CompilerParams(dimension_semantics=("arbitrary",),
                                             vmem_limit_bytes=V7X_VMEM_LIMIT_BYTES),
        name="mlp",
    )(xp, xs, up, down, vln)


def _block_diag(blocks):
    n, a, b = blocks.shape[-3:]
    eye = jnp.eye(n, dtype=blocks.dtype)
    out = jnp.einsum("...nab,nm->...namb", blocks, eye)
    return out.reshape(*blocks.shape[:-3], n * a, n * b)


def kernel(x_prompt, x_sample, state_lru_conv, state_lru_h, state_sconv, state_ssm_re, state_ssm_im,
           w_in, conv_a_w, conv_a_b, gate_x_w, gate_x_b, gate_a_w, gate_a_b, lru_lambda, conv_b_w,
           ssm_a_re, ssm_a_im, ssm_log_dt, ssm_b_re, ssm_b_im, ssm_c_re, ssm_c_im, ssm_d, glu_w, glu_b,
           proj_a, proj_b, proj_c, w_out, ln1_g, ln1_b, mlp_up, mlp_down, ln2_g, ln2_b):
    batch_p, batch_s = x_prompt.shape[0], x_sample.shape[0]
    bf = lambda w: w.astype(BF16)

    vab, bbar_re, bbar_im = _s5_prep(ssm_a_re, ssm_a_im, ssm_log_dt, ssm_b_re, ssm_b_im)

    def b_blocks(bbar):
        bb = bbar.reshape(DEPTH, SSM_GROUP, WIDTH // B_BLK, B_BLK // SSM_GROUP, SSM_STATE)
        return _block_diag(jnp.transpose(bb, (0, 2, 3, 1, 4)))

    bblk = bf(jnp.concatenate([b_blocks(bbar_re), b_blocks(bbar_im)], axis=-1))

    def c_blocks(c):
        cc = c.reshape(DEPTH, NS // C_BLK, C_BLK // SSM_STATE, SSM_GROUP, SSM_STATE)
        return bf(_block_diag(jnp.swapaxes(cc, -1, -2)))

    cre, cim = c_blocks(ssm_c_re), c_blocks(-ssm_c_im)

    def gate_blocks(w):
        heads = GATE_BLK // LRU_HEAD_DIM
        return bf(_block_diag(w.reshape(DEPTH, LRU_HEADS // heads, heads, LRU_HEAD_DIM, LRU_HEAD_DIM)))

    gxw, gaw = gate_blocks(gate_x_w), gate_blocks(gate_a_w)

    row = lambda v: v[:, None, :]
    vw = jnp.concatenate([conv_a_w, row(conv_a_b), row(gate_x_b), row(gate_a_b), row(lru_lambda),
                          conv_b_w, row(ssm_d), row(glu_b), jnp.zeros((DEPTH, 3, WIDTH), F32)], axis=1)
    vln1 = jnp.stack([ln1_g, ln1_b], axis=1)
    vln2 = jnp.stack([ln2_g, ln2_b], axis=1)
    wts = (bf(w_in), gxw, gaw, bf(proj_a), bf(proj_b), bf(proj_c), bf(glu_w), bf(w_out), bblk, cre, cim,
           vw, vln1, vab)
    up, down = mlp_up, mlp_down

    states_s = (state_lru_conv.reshape(DEPTH, batch_s, (LRU_CONV - 1) * WIDTH), state_lru_h,
                state_sconv.reshape(DEPTH, batch_s, (SC_CONV - 1) * WIDTH),
                state_ssm_re.reshape(DEPTH, batch_s, NS), state_ssm_im.reshape(DEPTH, batch_s, NS))

    yp = x_prompt.reshape(batch_p, -1)
    ys = x_sample.reshape(batch_s, -1)
    new_p, new_s = [], []
    for layer in range(DEPTH):
        yp, stp = _mixer(yp, layer, None, wts)
        ys, sts = _mixer(ys, layer, states_s, wts)
        yp, ys = _mlp(yp, ys, layer, up, down, vln2)
        new_p.append(stp)
        new_s.append(sts)

    def collect(new, batch):
        ca, ha, cb, sre, sim = (jnp.stack([st[j] for st in new], 0) for j in range(5))
        shape_ssm = (DEPTH, batch, SSM_GROUPS, SSM_STATE)
        return (ca.reshape(DEPTH, batch, LRU_CONV - 1, WIDTH), ha, cb.reshape(DEPTH, batch, SC_CONV - 1, WIDTH),
                sre.reshape(shape_ssm), sim.reshape(shape_ssm))

    return ((yp.reshape(x_prompt.shape), ys.reshape(x_sample.shape))
            + collect(new_p, batch_p) + collect(new_s, batch_s))
```

```python
import functools

import jax
import jax.numpy as jnp
from jax import lax
from jax.experimental import pallas as pl
from jax.experimental.pallas import tpu as pltpu

F32 = jnp.float32
BF16 = jnp.bfloat16

D_MODEL = 1024
DEPTH = 2
WIDTH = 512
LRU_HEADS = 8
LRU_HEAD_DIM = WIDTH // LRU_HEADS
LRU_CONV = 4
LRU_C = 8.0
SC_CONV = 3
SSM_GROUP = 16
SSM_GROUPS = WIDTH // SSM_GROUP
SSM_STATE = 64
NS = SSM_GROUPS * SSM_STATE
D_FF = 4 * D_MODEL
N_IN = 6 * WIDTH + 3 * D_MODEL
ALPHA = (2 * DEPTH) ** 0.25
LN_EPS = 1e-5

OFF_XA, OFF_YA, OFF_SB, OFF_SC, OFF_SH, OFF_US, OFF_GL = 0, 512, 1024, 1536, 2048, 2560, 3072

ROW_TILE = 512
MIXER_ROW_TILE = 512
GATE_BLK = 256
B_BLK = 128
C_BLK = 512
FF_BLK = 1024
N_WTS = 15
TAIL_PARTS = 4
TIME_PARTS = 2

ROW_CAW, ROW_CAB, ROW_GXB, ROW_GAB, ROW_LAM, ROW_CBW, ROW_SSD, ROW_GLB = 0, 4, 5, 6, 7, 8, 11, 12

V7X_VMEM_LIMIT_BYTES = 58 * 1024 * 1024


def _dot(a, b):
    return jnp.dot(a, b, preferred_element_type=F32)


def _layer_norm(x, g, b):
    mu = jnp.mean(x, axis=-1, keepdims=True)
    xc = x - mu
    var = jnp.mean(xc * xc, axis=-1, keepdims=True)
    return xc * lax.rsqrt(var + LN_EPS) * g + b


def _load_rows(x_ref, t0=0, t1=None):
    t1 = x_ref.shape[1] // D_MODEL if t1 is None else t1
    return jnp.concatenate([x_ref[:, t * D_MODEL:(t + 1) * D_MODEL] for t in range(t0, t1)], axis=0)


def _store_rows(o_ref, rows, t0=0):
    r = o_ref.shape[0]
    for t in range(rows.shape[0] // r):
        o_ref[:, (t0 + t) * D_MODEL:(t0 + t + 1) * D_MODEL] = rows[t * r:(t + 1) * r, :]


def _s5_prep_kernel(a_re_ref, a_im_ref, log_dt_ref, b_re_ref, b_im_ref,
                    abar_ref, bbar_re_ref, bbar_im_ref):
    a_re = a_re_ref[...]
    a_im = a_im_ref[...]
    step = jnp.exp(log_dt_ref[...])
    mag = jnp.exp(step * a_re)
    abar_re = mag * jnp.cos(step * a_im)
    abar_im = mag * jnp.sin(step * a_im)
    den = a_re * a_re + a_im * a_im
    nr = abar_re - 1.0
    ni = abar_im
    coef_re = (nr * a_re + ni * a_im) / den
    coef_im = (ni * a_re - nr * a_im) / den
    b_re = b_re_ref[...]
    b_im = b_im_ref[...]
    abar_ref[:, 0:1, :] = abar_re
    abar_ref[:, 1:2, :] = abar_im
    bbar_re_ref[...] = coef_re * b_re - coef_im * b_im
    bbar_im_ref[...] = coef_re * b_im + coef_im * b_re


def _s5_prep(ssm_a_re, ssm_a_im, ssm_log_dt, ssm_b_re, ssm_b_im):
    flat = lambda a: a.reshape(DEPTH, 1, NS)
    chan_major = lambda b: jnp.transpose(b, (0, 3, 1, 2)).reshape(DEPTH, SSM_GROUP, NS)
    vec = jax.ShapeDtypeStruct((DEPTH, 2, NS), F32)
    mat = jax.ShapeDtypeStruct((DEPTH, SSM_GROUP, NS), F32)
    return pl.pallas_call(_s5_prep_kernel, out_shape=(vec, mat, mat), name="s5_prep")(
        flat(ssm_a_re), flat(ssm_a_im), flat(ssm_log_dt), chan_major(ssm_b_re), chan_major(ssm_b_im))


def _mixer_kernel(R, has_init, aliased, *refs):
    n_state = 5 if has_init else 0
    x_ref = refs[0]
    states_in = refs[1:1 + n_state]
    (w_in, gxw, gaw, pa, pb, pc, gluw, wout, bre, bim, cre, cim, vw, vln, vab) = refs[1 + n_state:1 + n_state + N_WTS]
    n_in = 1 + n_state + N_WTS + (5 if aliased else 0)
    x1_ref, ca_out, ha_out, cb_out, sre_out, sim_out = refs[n_in:n_in + 6]
    xa_ext, u_ext, us_buf, bu, ha, sre, sim = refs[n_in + 6:]
    T = MIXER_ROW_TILE // R
    TR = MIXER_ROW_TILE
    KA = (LRU_CONV - 1) * R
    KB = (SC_CONV - 1) * R
    i = pl.program_id(0)

    @pl.when(i == 0)
    def _():
        if has_init:
            ca_in, ha_in, cb_in, sre_in, sim_in = states_in
            for k in range(LRU_CONV - 1):
                xa_ext[k * R:(k + 1) * R, :] = ca_in[:, k * WIDTH:(k + 1) * WIDTH]
            for k in range(SC_CONV - 1):
                u_ext[k * R:(k + 1) * R, :] = cb_in[:, k * WIDTH:(k + 1) * WIDTH]
            ha[...] = ha_in[...]
            sre[...] = sre_in[...]
            sim[...] = sim_in[...]
        else:
            xa_ext[0:KA, :] = jnp.zeros((KA, WIDTH), F32)
            u_ext[0:KB, :] = jnp.zeros((KB, WIDTH), F32)
            ha[...] = jnp.zeros((R, WIDTH), F32)
            sre[...] = jnp.zeros((R, NS), F32)
            sim[...] = jnp.zeros((R, NS), F32)

    xb = _load_rows(x_ref).astype(BF16)

    def proj(lo, hi):
        return _dot(xb, w_in[:, lo:hi])

    def vrow(r, sl=slice(None)):
        return vw[r:r + 1, sl]

    parts = [(p * T // TIME_PARTS, (p + 1) * T // TIME_PARTS) for p in range(TIME_PARTS)]

    def taps(ext, w_row, n_taps, t0, t1):
        acc = ext[t0 * R:t1 * R, :] * vrow(w_row)
        for k in range(1, n_taps):
            acc = acc + ext[(t0 + k) * R:(t1 + k) * R, :] * vrow(w_row + k)
        return acc

    us_buf[...] = proj(OFF_US, OFF_GL)
    za = []
    for m in range(WIDTH // B_BLK):
        usb = us_buf[:, m * B_BLK:(m + 1) * B_BLK].astype(BF16)
        bu[:, m * C_BLK:(m + 1) * C_BLK] = _dot(usb, bre[m])
        bu[:, NS + m * C_BLK:NS + (m + 1) * C_BLK] = _dot(usb, bim[m])
        za.append(proj(OFF_XA + m * 2 * B_BLK, OFF_XA + (m + 1) * 2 * B_BLK))
    za = jnp.concatenate(za, axis=1)
    xa_ext[KA:KA + TR, :] = za[:, :WIDTH]
    zb = proj(OFF_SB, OFF_US)
    u_ext[KB:KB + TR, :] = zb[:, WIDTH:2 * WIDTH] * zb[:, 2 * WIDTH:]

    softplus_neg_lam = jax.nn.softplus(-vrow(ROW_LAM))
    gate_in = []
    for t0, t1 in parts:
        xa_c = taps(xa_ext, ROW_CAW, LRU_CONV, t0, t1) + vrow(ROW_CAB)
        pre = []
        for hh in range(WIDTH // GATE_BLK):
            xcb = xa_c[:, hh * GATE_BLK:(hh + 1) * GATE_BLK].astype(BF16)
            pre.append((_dot(xcb, gxw[hh]), _dot(xcb, gaw[hh])))
        gate_in.append((xa_c, pre))

    h_lru = ha[...]
    lhs_a = []
    for (t0, t1), (xa_c, pre) in zip(parts, gate_in):
        a_in, b_in = [], []
        for hh in range(WIDTH // GATE_BLK):
            sl = slice(hh * GATE_BLK, (hh + 1) * GATE_BLK)
            gx = jax.nn.sigmoid(pre[hh][0] + vrow(ROW_GXB, sl))
            ga = jax.nn.sigmoid(pre[hh][1] + vrow(ROW_GAB, sl))
            a = jnp.exp(-LRU_C * ga * softplus_neg_lam[:, sl])
            a_in.append(a)
            b_in.append(jnp.sqrt(1.0 - a * a) * (gx * xa_c[:, sl]))
        a_in, b_in = jnp.concatenate(a_in, axis=1), jnp.concatenate(b_in, axis=1)
        h_rows = []
        for t in range(t1 - t0):
            h_lru = a_in[t * R:(t + 1) * R, :] * h_lru + b_in[t * R:(t + 1) * R, :]
            h_rows.append(h_lru)
        h_a = jnp.concatenate(h_rows, axis=0)
        lhs_a.append((h_a * jax.nn.gelu(za[t0 * R:t1 * R, WIDTH:])).astype(BF16))
    ha[...] = h_lru

    lhs_b = [(zb[t0 * R:t1 * R, :WIDTH] * taps(u_ext, ROW_CBW, SC_CONV, t0, t1)).astype(BF16) for t0, t1 in parts]

    slab = max(128, min(512, 4096 // R))
    slabs = [(slice(lo, lo + slab), slice(NS + lo, NS + lo + slab)) for lo in range(0, NS, slab)]
    abar = [(jnp.broadcast_to(vab[0:1, sl], (R, slab)), jnp.broadcast_to(vab[1:2, sl], (R, slab))) for sl, _ in slabs]
    h_s5 = [(sre[:, sl], sim[:, sl]) for sl, _ in slabs]
    for t0, t1 in parts:
        for s, (sl, sl_im) in enumerate(slabs):
            (ar, ai), (hr, hi) = abar[s], h_s5[s]
            for t in range(t0, t1):
                step = slice(t * R, (t + 1) * R)
                hr, hi = ar * hr - ai * hi + bu[step, sl], ar * hi + ai * hr + bu[step, sl_im]
                bu[step, sl] = hr
                bu[step, sl_im] = hi
            h_s5[s] = (hr, hi)
    for (sl, _), (hr, hi) in zip(slabs, h_s5):
        sre[:, sl] = hr
        sim[:, sl] = hi

    gates, out_ab, zc, out_c = {}, {}, {}, {}

    def stage_gates(p):
        t0, t1 = parts[p]
        xb_p = xb[t0 * R:t1 * R, :]
        gates[p] = [jax.nn.sigmoid(_dot(xb_p, w_in[:, OFF_GL + j * D_MODEL:OFF_GL + (j + 1) * D_MODEL]))
                    for j in range(3)]

    def stage_branches(p):
        t0, t1 = parts[p]
        rows = slice(t0 * R, t1 * R)
        out_ab[p] = (_dot(lhs_a[p], pa[...]), _dot(lhs_b[p], pb[...]))
        y = []
        for k in range(NS // C_BLK):
            hre = bu[rows, k * C_BLK:(k + 1) * C_BLK].astype(BF16)
            him = bu[rows, NS + k * C_BLK:NS + (k + 1) * C_BLK].astype(BF16)
            sl = slice(k * 128, (k + 1) * 128)
            y.append(_dot(hre, cre[k]) + _dot(him, cim[k]) + vrow(ROW_SSD, sl) * us_buf[rows, sl])
        zc[p] = jax.nn.gelu(jnp.concatenate(y, axis=1))

    def stage_glu(p):
        zc[p] = (zc[p] * jax.nn.sigmoid(_dot(zc[p].astype(BF16), gluw[...]) + vrow(ROW_GLB))).astype(BF16)

    def stage_proj_c(p):
        out_c[p] = _dot(zc[p], pc[...])

    def stage_out(p):
        t0, t1 = parts[p]
        m = gates[p][0] * out_ab[p][0] + gates[p][1] * out_ab[p][1] + gates[p][2] * out_c[p]
        out = _dot(m.astype(BF16), wout[...])
        y1 = _layer_norm(ALPHA * _load_rows(x_ref, t0, t1) + out, vln[0:1, :], vln[1:2, :])
        _store_rows(x1_ref, y1, t0)

    for stage in (stage_gates, stage_branches, stage_glu, stage_proj_c, stage_out):
        for p in range(TIME_PARTS):
            stage(p)

    xa_ext[0:KA, :] = xa_ext[TR:TR + KA, :]
    u_ext[0:KB, :] = u_ext[TR:TR + KB, :]

    @pl.when(i == pl.num_programs(0) - 1)
    def _():
        for k in range(LRU_CONV - 1):
            ca_out[:, k * WIDTH:(k + 1) * WIDTH] = xa_ext[k * R:(k + 1) * R, :]
        for k in range(SC_CONV - 1):
            cb_out[:, k * WIDTH:(k + 1) * WIDTH] = u_ext[k * R:(k + 1) * R, :]
        ha_out[...] = ha[...]
        sre_out[...] = sre[...]
        sim_out[...] = sim[...]


def _layer_block(layer, shape):
    zeros = (0,) * (len(shape) - 1)
    return pl.BlockSpec((None,) + tuple(shape[1:]), lambda i: (layer,) + zeros, pipeline_mode=pl.Buffered(1))


def _mixer(x, layer, states, wts, prev_states):
    R = x.shape[0]
    tile = MIXER_ROW_TILE
    T = tile // R
    n_steps = x.shape[1] // (T * D_MODEL)
    assert tile % R == 0 and x.shape[1] % (T * D_MODEL) == 0
    x_spec = pl.BlockSpec((R, T * D_MODEL), lambda i: (0, i))
    state_shapes = [(R, (LRU_CONV - 1) * WIDTH), (R, WIDTH), (R, (SC_CONV - 1) * WIDTH), (R, NS), (R, NS)]
    states = () if states is None else tuple(states)
    prev_states = () if prev_states is None else tuple(prev_states)
    assert len(wts) == N_WTS
    in_specs = ([x_spec] + [_layer_block(layer, s.shape) for s in states]
                + [_layer_block(layer, w.shape) for w in wts]
                + [pl.BlockSpec(memory_space=pl.ANY) for _ in prev_states])
    out_shape = ([jax.ShapeDtypeStruct(x.shape, F32)]
                 + [jax.ShapeDtypeStruct((DEPTH,) + s, F32) for s in state_shapes])
    out_specs = [x_spec] + [pl.BlockSpec((None,) + s, lambda i: (layer, 0, 0)) for s in state_shapes]
    first_prev = 1 + len(states) + len(wts)
    aliases = {first_prev + j: 1 + j for j in range(len(prev_states))}
    scratch = [
        pltpu.VMEM(((LRU_CONV - 1) * R + tile, WIDTH), F32),
        pltpu.VMEM(((SC_CONV - 1) * R + tile, WIDTH), F32),
        pltpu.VMEM((tile, WIDTH), F32),
        pltpu.VMEM((tile, 2 * NS), F32),
        pltpu.VMEM((R, WIDTH), F32),
        pltpu.VMEM((R, NS), F32),
        pltpu.VMEM((R, NS), F32),
    ]
    outs = pl.pallas_call(
        functools.partial(_mixer_kernel, R, bool(states), bool(prev_states)),
        grid=(n_steps,), in_specs=in_specs, out_specs=out_specs, out_shape=out_shape,
        scratch_shapes=scratch, input_output_aliases=aliases,
        compiler_params=pltpu.CompilerParams(dimension_semantics=("arbitrary",),
                                             vmem_limit_bytes=V7X_VMEM_LIMIT_BYTES),
        name=f"mixer_r{R}",
    )(x, *states, *wts, *prev_states)
    return outs[0], outs[1:]


def _mlp_block(x_ref, o_ref, up, down, vln):
    x = _load_rows(x_ref)
    xb = x.astype(BF16)
    acc = ALPHA * x
    n_ff = D_FF // FF_BLK
    for c in range(n_ff):
        h = _dot(xb, up[:, c * FF_BLK:(c + 1) * FF_BLK].astype(BF16))
        h = jnp.square(jnp.maximum(h, 0.0)).astype(BF16)
        if c < n_ff - 1:
            acc = acc + _dot(h, down[c * FF_BLK:(c + 1) * FF_BLK, :].astype(BF16))
    r = x_ref.shape[0]
    steps = x.shape[0] // r
    down_last = down[(n_ff - 1) * FF_BLK:, :].astype(BF16)
    for part in range(TAIL_PARTS):
        t0, t1 = part * steps // TAIL_PARTS, (part + 1) * steps // TAIL_PARTS
        rows = slice(t0 * r, t1 * r)
        f = acc[rows, :] + _dot(h[rows, :], down_last)
        _store_rows(o_ref, _layer_norm(f, vln[0:1, :], vln[1:2, :]), t0)


def _mlp_kernel(n_prompt, xp_ref, xs_ref, up, down, vln, op_ref, os_ref):
    i = pl.program_id(0)

    @pl.when(i < n_prompt)
    def _():
        _mlp_block(xp_ref, op_ref, up, down, vln)

    @pl.when(i == n_prompt)
    def _():
        _mlp_block(xs_ref, os_ref, up, down, vln)


def _mlp(xp, xs, layer, up, down, vln):
    rp, rs = xp.shape[0], xs.shape[0]
    tp = ROW_TILE // rp
    n_prompt = xp.shape[1] // (tp * D_MODEL)
    assert xs.shape[1] * rs == ROW_TILE * D_MODEL
    p_spec = pl.BlockSpec((rp, tp * D_MODEL), lambda i: (0, jnp.minimum(i, n_prompt - 1)))
    s_spec = pl.BlockSpec(xs.shape, lambda i: (0, 0))
    return pl.pallas_call(
        functools.partial(_mlp_kernel, n_prompt),
        grid=(n_prompt + 1,),
        in_specs=[p_spec, s_spec] + [_layer_block(layer, w.shape) for w in (up, down, vln)],
        out_specs=[p_spec, s_spec],
        out_shape=[jax.ShapeDtypeStruct(xp.shape, F32), jax.ShapeDtypeStruct(xs.shape, F32)],
        compiler_params=pltpu.CompilerParams(dimension_semantics=("arbitrary",),
                                             vmem_limit_bytes=V7X_VMEM_LIMIT_BYTES),
        name="mlp",
    )(xp, xs, up, down, vln)


def _block_diag(blocks):
    n, a, b = blocks.shape[-3:]
    eye = jnp.eye(n, dtype=blocks.dtype)
    out = blocks[..., :, :, None, :] * eye[:, None, :, None]
    return out.reshape(*blocks.shape[:-3], n * a, n * b)


def kernel(x_prompt, x_sample, state_lru_conv, state_lru_h, state_sconv, state_ssm_re, state_ssm_im,
           w_in, conv_a_w, conv_a_b, gate_x_w, gate_x_b, gate_a_w, gate_a_b, lru_lambda, conv_b_w,
           ssm_a_re, ssm_a_im, ssm_log_dt, ssm_b_re, ssm_b_im, ssm_c_re, ssm_c_im, ssm_d, glu_w, glu_b,
           proj_a, proj_b, proj_c, w_out, ln1_g, ln1_b, mlp_up, mlp_down, ln2_g, ln2_b):
    batch_p, batch_s = x_prompt.shape[0], x_sample.shape[0]
    bf = lambda w: w.astype(BF16)

    vab, bbar_re, bbar_im = _s5_prep(ssm_a_re, ssm_a_im, ssm_log_dt, ssm_b_re, ssm_b_im)

    def b_blocks(bbar):
        bb = bbar.reshape(DEPTH, SSM_GROUP, WIDTH // B_BLK, B_BLK // SSM_GROUP, SSM_STATE)
        return bf(_block_diag(jnp.transpose(bb, (0, 2, 3, 1, 4))))

    def c_blocks(c):
        cc = c.reshape(DEPTH, NS // C_BLK, C_BLK // SSM_STATE, SSM_GROUP, SSM_STATE)
        return bf(_block_diag(jnp.swapaxes(cc, -1, -2)))

    cre, cim = c_blocks(ssm_c_re), c_blocks(-ssm_c_im)

    def gate_blocks(w):
        heads = GATE_BLK // LRU_HEAD_DIM
        return bf(_block_diag(w.reshape(DEPTH, LRU_HEADS // heads, heads, LRU_HEAD_DIM, LRU_HEAD_DIM)))

    gxw, gaw = gate_blocks(gate_x_w), gate_blocks(gate_a_w)

    row = lambda v: v[:, None, :]
    vw = jnp.concatenate([conv_a_w, row(conv_a_b), row(gate_x_b), row(gate_a_b), row(lru_lambda),
                          conv_b_w, row(ssm_d), row(glu_b), jnp.zeros((DEPTH, 3, WIDTH), F32)], axis=1)
    vln1 = jnp.stack([ln1_g, ln1_b], axis=1)
    vln2 = jnp.stack([ln2_g, ln2_b], axis=1)
    wts = (bf(w_in), gxw, gaw, bf(proj_a), bf(proj_b), bf(proj_c), bf(glu_w), bf(w_out),
           b_blocks(bbar_re), b_blocks(bbar_im), cre, cim, vw, vln1, vab)
    up, down = mlp_up, mlp_down

    states_s = (state_lru_conv.reshape(DEPTH, batch_s, (LRU_CONV - 1) * WIDTH), state_lru_h,
                state_sconv.reshape(DEPTH, batch_s, (SC_CONV - 1) * WIDTH),
                state_ssm_re.reshape(DEPTH, batch_s, NS), state_ssm_im.reshape(DEPTH, batch_s, NS))

    yp = x_prompt.reshape(batch_p, -1)
    ys = x_sample.reshape(batch_s, -1)
    new_p = new_s = None
    for layer in range(DEPTH):
        yp, new_p = _mixer(yp, layer, None, wts, new_p)
        ys, new_s = _mixer(ys, layer, states_s, wts, new_s)
        yp, ys = _mlp(yp, ys, layer, up, down, vln2)

    def collect(new, batch):
        ca, ha, cb, sre, sim = new
        shape_ssm = (DEPTH, batch, SSM_GROUPS, SSM_STATE)
        return (ca.reshape(DEPTH, batch, LRU_CONV - 1, WIDTH), ha, cb.reshape(DEPTH, batch, SC_CONV - 1, WIDTH),
                sre.reshape(shape_ssm), sim.reshape(shape_ssm))

    return ((yp.reshape(x_prompt.shape), ys.reshape(x_sample.shape))
            + collect(new_p, batch_p) + collect(new_s, batch_s))
```

```python
import functools

import jax
import jax.numpy as jnp
from jax import lax
from jax.experimental import pallas as pl
from jax.experimental.pallas import tpu as pltpu

F32 = jnp.float32
BF16 = jnp.bfloat16

D_MODEL = 1024
DEPTH = 2
WIDTH = 512
LRU_HEADS = 8
LRU_HEAD_DIM = WIDTH // LRU_HEADS
LRU_CONV = 4
LRU_C = 8.0
SC_CONV = 3
SSM_GROUP = 16
SSM_GROUPS = WIDTH // SSM_GROUP
SSM_STATE = 64
NS = SSM_GROUPS * SSM_STATE
D_FF = 4 * D_MODEL
N_IN = 6 * WIDTH + 3 * D_MODEL
ALPHA = (2 * DEPTH) ** 0.25
LN_EPS = 1e-5

OFF_XA, OFF_YA, OFF_SB, OFF_SC, OFF_SH, OFF_US, OFF_GL = 0, 512, 1024, 1536, 2048, 2560, 3072

ROW_TILE = 512
MIXER_ROW_TILE = 512
GATE_BLK = 256
B_BLK = 128
C_BLK = 512
FF_BLK = 1024
N_WTS = 15
TAIL_PARTS = 4
TIME_PARTS = 2

ROW_CAW, ROW_CAB, ROW_GXB, ROW_GAB, ROW_LAM, ROW_CBW, ROW_SSD, ROW_GLB = 0, 4, 5, 6, 7, 8, 11, 12

V7X_VMEM_LIMIT_BYTES = 58 * 1024 * 1024


def _dot(a, b):
    return jnp.dot(a, b, preferred_element_type=F32)


def _layer_norm(x, g, b):
    mu = jnp.mean(x, axis=-1, keepdims=True)
    xc = x - mu
    var = jnp.mean(xc * xc, axis=-1, keepdims=True)
    return xc * lax.rsqrt(var + LN_EPS) * g + b


def _load_rows(x_ref, t0=0, t1=None):
    t1 = x_ref.shape[1] // D_MODEL if t1 is None else t1
    return jnp.concatenate([x_ref[:, t * D_MODEL:(t + 1) * D_MODEL] for t in range(t0, t1)], axis=0)


def _store_rows(o_ref, rows, t0=0):
    r = o_ref.shape[0]
    for t in range(rows.shape[0] // r):
        o_ref[:, (t0 + t) * D_MODEL:(t0 + t + 1) * D_MODEL] = rows[t * r:(t + 1) * r, :]


def _s5_prep_kernel(a_re_ref, a_im_ref, log_dt_ref, b_re_ref, b_im_ref,
                    abar_ref, bbar_re_ref, bbar_im_ref):
    a_re = a_re_ref[...]
    a_im = a_im_ref[...]
    step = jnp.exp(log_dt_ref[...])
    mag = jnp.exp(step * a_re)
    abar_re = mag * jnp.cos(step * a_im)
    abar_im = mag * jnp.sin(step * a_im)
    den = a_re * a_re + a_im * a_im
    nr = abar_re - 1.0
    ni = abar_im
    coef_re = (nr * a_re + ni * a_im) / den
    coef_im = (ni * a_re - nr * a_im) / den
    b_re = b_re_ref[...]
    b_im = b_im_ref[...]
    abar_ref[:, 0:1, :] = abar_re
    abar_ref[:, 1:2, :] = abar_im
    bbar_re_ref[...] = coef_re * b_re - coef_im * b_im
    bbar_im_ref[...] = coef_re * b_im + coef_im * b_re


def _s5_prep(ssm_a_re, ssm_a_im, ssm_log_dt, ssm_b_re, ssm_b_im):
    flat = lambda a: a.reshape(DEPTH, 1, NS)
    chan_major = lambda b: jnp.transpose(b, (0, 3, 1, 2)).reshape(DEPTH, SSM_GROUP, NS)
    vec = jax.ShapeDtypeStruct((DEPTH, 2, NS), F32)
    mat = jax.ShapeDtypeStruct((DEPTH, SSM_GROUP, NS), F32)
    return pl.pallas_call(_s5_prep_kernel, out_shape=(vec, mat, mat), name="s5_prep")(
        flat(ssm_a_re), flat(ssm_a_im), flat(ssm_log_dt), chan_major(ssm_b_re), chan_major(ssm_b_im))


def _mixer_kernel(R, has_init, *refs):
    n_state = 5 if has_init else 0
    x_ref = refs[0]
    states_in = refs[1:1 + n_state]
    (w_in, gxw, gaw, pa, pb, pc, gluw, wout, bre, bim, cre, cim, vw, vln, vab) = refs[1 + n_state:1 + n_state + N_WTS]
    n_in = 1 + n_state + N_WTS
    x1_ref, ca_out, ha_out, cb_out, sre_out, sim_out = refs[n_in:n_in + 6]
    xa_ext, u_ext, us_buf, bu, ha, sre, sim = refs[n_in + 6:]
    T = MIXER_ROW_TILE // R
    TR = MIXER_ROW_TILE
    KA = (LRU_CONV - 1) * R
    KB = (SC_CONV - 1) * R
    i = pl.program_id(0)

    @pl.when(i == 0)
    def _():
        if has_init:
            ca_in, ha_in, cb_in, sre_in, sim_in = states_in
            for k in range(LRU_CONV - 1):
                xa_ext[k * R:(k + 1) * R, :] = ca_in[:, k * WIDTH:(k + 1) * WIDTH]
            for k in range(SC_CONV - 1):
                u_ext[k * R:(k + 1) * R, :] = cb_in[:, k * WIDTH:(k + 1) * WIDTH]
            ha[...] = ha_in[...]
            sre[...] = sre_in[...]
            sim[...] = sim_in[...]
        else:
            xa_ext[0:KA, :] = jnp.zeros((KA, WIDTH), F32)
            u_ext[0:KB, :] = jnp.zeros((KB, WIDTH), F32)
            ha[...] = jnp.zeros((R, WIDTH), F32)
            sre[...] = jnp.zeros((R, NS), F32)
            sim[...] = jnp.zeros((R, NS), F32)

    xb = _load_rows(x_ref).astype(BF16)

    def proj(lo, hi):
        return _dot(xb, w_in[:, lo:hi])

    def vrow(r, sl=slice(None)):
        return vw[r:r + 1, sl]

    parts = [(p * T // TIME_PARTS, (p + 1) * T // TIME_PARTS) for p in range(TIME_PARTS)]

    def taps(ext, w_row, n_taps, t0, t1):
        acc = ext[t0 * R:t1 * R, :] * vrow(w_row)
        for k in range(1, n_taps):
            acc = acc + ext[(t0 + k) * R:(t1 + k) * R, :] * vrow(w_row + k)
        return acc

    us_buf[...] = proj(OFF_US, OFF_GL)
    za = []
    for m in range(WIDTH // B_BLK):
        usb = us_buf[:, m * B_BLK:(m + 1) * B_BLK].astype(BF16)
        bu[:, m * C_BLK:(m + 1) * C_BLK] = _dot(usb, bre[m])
        bu[:, NS + m * C_BLK:NS + (m + 1) * C_BLK] = _dot(usb, bim[m])
        za.append(proj(OFF_XA + m * 2 * B_BLK, OFF_XA + (m + 1) * 2 * B_BLK))
    za = jnp.concatenate(za, axis=1)
    xa_ext[KA:KA + TR, :] = za[:, :WIDTH]
    zb = proj(OFF_SB, OFF_US)
    u_ext[KB:KB + TR, :] = zb[:, WIDTH:2 * WIDTH] * zb[:, 2 * WIDTH:]

    softplus_neg_lam = jax.nn.softplus(-vrow(ROW_LAM))
    gate_in = []
    for t0, t1 in parts:
        xa_c = taps(xa_ext, ROW_CAW, LRU_CONV, t0, t1) + vrow(ROW_CAB)
        pre = []
        for hh in range(WIDTH // GATE_BLK):
            xcb = xa_c[:, hh * GATE_BLK:(hh + 1) * GATE_BLK].astype(BF16)
            pre.append((_dot(xcb, gxw[hh]), _dot(xcb, gaw[hh])))
        gate_in.append((xa_c, pre))

    h_lru = ha[...]
    lhs_a = []
    for (t0, t1), (xa_c, pre) in zip(parts, gate_in):
        a_in, b_in = [], []
        for hh in range(WIDTH // GATE_BLK):
            sl = slice(hh * GATE_BLK, (hh + 1) * GATE_BLK)
            gx = jax.nn.sigmoid(pre[hh][0] + vrow(ROW_GXB, sl))
            ga = jax.nn.sigmoid(pre[hh][1] + vrow(ROW_GAB, sl))
            a = jnp.exp(-LRU_C * ga * softplus_neg_lam[:, sl])
            a_in.append(a)
            b_in.append(jnp.sqrt(1.0 - a * a) * (gx * xa_c[:, sl]))
        a_in, b_in = jnp.concatenate(a_in, axis=1), jnp.concatenate(b_in, axis=1)
        h_rows = []
        for t in range(t1 - t0):
            h_lru = a_in[t * R:(t + 1) * R, :] * h_lru + b_in[t * R:(t + 1) * R, :]
            h_rows.append(h_lru)
        h_a = jnp.concatenate(h_rows, axis=0)
        lhs_a.append((h_a * jax.nn.gelu(za[t0 * R:t1 * R, WIDTH:])).astype(BF16))
    ha[...] = h_lru

    lhs_b = [(zb[t0 * R:t1 * R, :WIDTH] * taps(u_ext, ROW_CBW, SC_CONV, t0, t1)).astype(BF16) for t0, t1 in parts]

    slab = max(128, min(512, 4096 // R))
    slabs = [(slice(lo, lo + slab), slice(NS + lo, NS + lo + slab)) for lo in range(0, NS, slab)]
    abar = [(jnp.broadcast_to(vab[0:1, sl], (R, slab)), jnp.broadcast_to(vab[1:2, sl], (R, slab))) for sl, _ in slabs]
    h_s5 = [(sre[:, sl], sim[:, sl]) for sl, _ in slabs]
    for t0, t1 in parts:
        for s, (sl, sl_im) in enumerate(slabs):
            (ar, ai), (hr, hi) = abar[s], h_s5[s]
            for t in range(t0, t1):
                step = slice(t * R, (t + 1) * R)
                hr, hi = ar * hr - ai * hi + bu[step, sl], ar * hi + ai * hr + bu[step, sl_im]
                bu[step, sl] = hr
                bu[step, sl_im] = hi
            h_s5[s] = (hr, hi)
    for (sl, _), (hr, hi) in zip(slabs, h_s5):
        sre[:, sl] = hr
        sim[:, sl] = hi

    gates, out_ab, zc, out_c = {}, {}, {}, {}

    def stage_gates(p):
        t0, t1 = parts[p]
        xb_p = xb[t0 * R:t1 * R, :]
        gates[p] = [jax.nn.sigmoid(_dot(xb_p, w_in[:, OFF_GL + j * D_MODEL:OFF_GL + (j + 1) * D_MODEL]))
                    for j in range(3)]

    def stage_branches(p):
        t0, t1 = parts[p]
        rows = slice(t0 * R, t1 * R)
        out_ab[p] = (_dot(lhs_a[p], pa[...]), _dot(lhs_b[p], pb[...]))
        y = []
        for k in range(NS // C_BLK):
            hre = bu[rows, k * C_BLK:(k + 1) * C_BLK].astype(BF16)
            him = bu[rows, NS + k * C_BLK:NS + (k + 1) * C_BLK].astype(BF16)
            sl = slice(k * 128, (k + 1) * 128)
            y.append(_dot(hre, cre[k]) + _dot(him, cim[k]) + vrow(ROW_SSD, sl) * us_buf[rows, sl])
        zc[p] = jax.nn.gelu(jnp.concatenate(y, axis=1))

    def stage_glu(p):
        zc[p] = (zc[p] * jax.nn.sigmoid(_dot(zc[p].astype(BF16), gluw[...]) + vrow(ROW_GLB))).astype(BF16)

    def stage_proj_c(p):
        out_c[p] = _dot(zc[p], pc[...])

    def stage_out(p):
        t0, t1 = parts[p]
        m = gates[p][0] * out_ab[p][0] + gates[p][1] * out_ab[p][1] + gates[p][2] * out_c[p]
        out = _dot(m.astype(BF16), wout[...])
        y1 = _layer_norm(ALPHA * _load_rows(x_ref, t0, t1) + out, vln[0:1, :], vln[1:2, :])
        _store_rows(x1_ref, y1, t0)

    for stage in (stage_gates, stage_branches, stage_glu, stage_proj_c, stage_out):
        for p in range(TIME_PARTS):
            stage(p)

    xa_ext[0:KA, :] = xa_ext[TR:TR + KA, :]
    u_ext[0:KB, :] = u_ext[TR:TR + KB, :]

    @pl.when(i == pl.num_programs(0) - 1)
    def _():
        for k in range(LRU_CONV - 1):
            ca_out[:, k * WIDTH:(k + 1) * WIDTH] = xa_ext[k * R:(k + 1) * R, :]
        for k in range(SC_CONV - 1):
            cb_out[:, k * WIDTH:(k + 1) * WIDTH] = u_ext[k * R:(k + 1) * R, :]
        ha_out[...] = ha[...]
        sre_out[...] = sre[...]
        sim_out[...] = sim[...]


def _layer_block(layer, shape):
    zeros = (0,) * (len(shape) - 1)
    return pl.BlockSpec((None,) + tuple(shape[1:]), lambda i: (layer,) + zeros, pipeline_mode=pl.Buffered(1))


def _mixer(x, layer, states, wts):
    R = x.shape[0]
    tile = MIXER_ROW_TILE
    T = tile // R
    n_steps = x.shape[1] // (T * D_MODEL)
    assert tile % R == 0 and x.shape[1] % (T * D_MODEL) == 0
    x_spec = pl.BlockSpec((R, T * D_MODEL), lambda i: (0, i))
    state_shapes = [(R, (LRU_CONV - 1) * WIDTH), (R, WIDTH), (R, (SC_CONV - 1) * WIDTH), (R, NS), (R, NS)]
    states = () if states is None else tuple(states)
    assert len(wts) == N_WTS
    in_specs = ([x_spec] + [_layer_block(layer, s.shape) for s in states]
                + [_layer_block(layer, w.shape) for w in wts])
    out_shape = [jax.ShapeDtypeStruct(x.shape, F32)] + [jax.ShapeDtypeStruct(s, F32) for s in state_shapes]
    out_specs = [x_spec] + [pl.BlockSpec(s, lambda i: (0, 0)) for s in state_shapes]
    scratch = [
        pltpu.VMEM(((LRU_CONV - 1) * R + tile, WIDTH), F32),
        pltpu.VMEM(((SC_CONV - 1) * R + tile, WIDTH), F32),
        pltpu.VMEM((tile, WIDTH), F32),
        pltpu.VMEM((tile, 2 * NS), F32),
        pltpu.VMEM((R, WIDTH), F32),
        pltpu.VMEM((R, NS), F32),
        pltpu.VMEM((R, NS), F32),
    ]
    outs = pl.pallas_call(
        functools.partial(_mixer_kernel, R, bool(states)),
        grid=(n_steps,), in_specs=in_specs, out_specs=out_specs, out_shape=out_shape,
        scratch_shapes=scratch,
        compiler_params=pltpu.CompilerParams(dimension_semantics=("arbitrary",),
                                             vmem_limit_bytes=V7X_VMEM_LIMIT_BYTES),
        name=f"mixer_r{R}",
    )(x, *states, *wts)
    return outs[0], outs[1:]


def _mlp_block(x_ref, o_ref, up, down, vln):
    x = _load_rows(x_ref)
    xb = x.astype(BF16)
    acc = ALPHA * x
    n_ff = D_FF // FF_BLK
    for c in range(n_ff):
        h = _dot(xb, up[:, c * FF_BLK:(c + 1) * FF_BLK].astype(BF16))
        h = jnp.square(jnp.maximum(h, 0.0)).astype(BF16)
        if c < n_ff - 1:
            acc = acc + _dot(h, down[c * FF_BLK:(c + 1) * FF_BLK, :].astype(BF16))
    r = x_ref.shape[0]
    steps = x.shape[0] // r
    down_last = down[(n_ff - 1) * FF_BLK:, :].astype(BF16)
    for part in range(TAIL_PARTS):
        t0, t1 = part * steps // TAIL_PARTS, (part + 1) * steps // TAIL_PARTS
        rows = slice(t0 * r, t1 * r)
        f = acc[rows, :] + _dot(h[rows, :], down_last)
        _store_rows(o_ref, _layer_norm(f, vln[0:1, :], vln[1:2, :]), t0)


def _mlp_kernel(n_prompt, xp_ref, xs_ref, up, down, vln, op_ref, os_ref):
    i = pl.program_id(0)

    @pl.when(i < n_prompt)
    def _():
        _mlp_block(xp_ref, op_ref, up, down, vln)

    @pl.when(i == n_prompt)
    def _():
        _mlp_block(xs_ref, os_ref, up, down, vln)


def _mlp(xp, xs, layer, up, down, vln):
    rp, rs = xp.shape[0], xs.shape[0]
    tp = ROW_TILE // rp
    n_prompt = xp.shape[1] // (tp * D_MODEL)
    assert xs.shape[1] * rs == ROW_TILE * D_MODEL
    p_spec = pl.BlockSpec((rp, tp * D_MODEL), lambda i: (0, jnp.minimum(i, n_prompt - 1)))
    s_spec = pl.BlockSpec(xs.shape, lambda i: (0, 0))
    return pl.pallas_call(
        functools.partial(_mlp_kernel, n_prompt),
        grid=(n_prompt + 1,),
        in_specs=[p_spec, s_spec] + [_layer_block(layer, w.shape) for w in (up, down, vln)],
        out_specs=[p_spec, s_spec],
        out_shape=[jax.ShapeDtypeStruct(xp.shape, F32), jax.ShapeDtypeStruct(xs.shape, F32)],
        compiler_params=pltpu.CompilerParams(dimension_semantics=("arbitrary",),
                                             vmem_limit_bytes=V7X_VMEM_LIMIT_BYTES),
        name="mlp",
    )(xp, xs, up, down, vln)


def _block_diag(blocks):
    n, a, b = blocks.shape[-3:]
    eye = jnp.eye(n, dtype=blocks.dtype)
    out = blocks[..., :, :, None, :] * eye[:, None, :, None]
    return out.reshape(*blocks.shape[:-3], n * a, n * b)


def kernel(x_prompt, x_sample, state_lru_conv, state_lru_h, state_sconv, state_ssm_re, state_ssm_im,
           w_in, conv_a_w, conv_a_b, gate_x_w, gate_x_b, gate_a_w, gate_a_b, lru_lambda, conv_b_w,
           ssm_a_re, ssm_a_im, ssm_log_dt, ssm_b_re, ssm_b_im, ssm_c_re, ssm_c_im, ssm_d, glu_w, glu_b,
           proj_a, proj_b, proj_c, w_out, ln1_g, ln1_b, mlp_up, mlp_down, ln2_g, ln2_b):
    batch_p, batch_s = x_prompt.shape[0], x_sample.shape[0]
    bf = lambda w: w.astype(BF16)

    vab, bbar_re, bbar_im = _s5_prep(ssm_a_re, ssm_a_im, ssm_log_dt, ssm_b_re, ssm_b_im)

    def b_blocks(bbar):
        bb = bbar.reshape(DEPTH, SSM_GROUP, WIDTH // B_BLK, B_BLK // SSM_GROUP, SSM_STATE)
        return bf(_block_diag(jnp.transpose(bb, (0, 2, 3, 1, 4))))

    def c_blocks(c):
        cc = c.reshape(DEPTH, NS // C_BLK, C_BLK // SSM_STATE, SSM_GROUP, SSM_STATE)
        return bf(_block_diag(jnp.swapaxes(cc, -1, -2)))

    cre, cim = c_blocks(ssm_c_re), c_blocks(-ssm_c_im)

    def gate_blocks(w):
        heads = GATE_BLK // LRU_HEAD_DIM
        return bf(_block_diag(w.reshape(DEPTH, LRU_HEADS // heads, heads, LRU_HEAD_DIM, LRU_HEAD_DIM)))

    gxw, gaw = gate_blocks(gate_x_w), gate_blocks(gate_a_w)

    row = lambda v: v[:, None, :]
    vw = jnp.concatenate([conv_a_w, row(conv_a_b), row(gate_x_b), row(gate_a_b), row(lru_lambda),
                          conv_b_w, row(ssm_d), row(glu_b)], axis=1)
    vln1 = jnp.stack([ln1_g, ln1_b], axis=1)
    vln2 = jnp.stack([ln2_g, ln2_b], axis=1)
    wts = (bf(w_in), gxw, gaw, bf(proj_a), bf(proj_b), bf(proj_c), bf(glu_w), bf(w_out),
           b_blocks(bbar_re), b_blocks(bbar_im), cre, cim, vw, vln1, vab)
    up, down = mlp_up, mlp_down

    states_s = (state_lru_conv.reshape(DEPTH, batch_s, (LRU_CONV - 1) * WIDTH), state_lru_h,
                state_sconv.reshape(DEPTH, batch_s, (SC_CONV - 1) * WIDTH),
                state_ssm_re.reshape(DEPTH, batch_s, NS), state_ssm_im.reshape(DEPTH, batch_s, NS))

    yp = x_prompt.reshape(batch_p, -1)
    ys = x_sample.reshape(batch_s, -1)
    new_p, new_s = [], []
    for layer in range(DEPTH):
        yp, stp = _mixer(yp, layer, None, wts)
        ys, sts = _mixer(ys, layer, states_s, wts)
        yp, ys = _mlp(yp, ys, layer, up, down, vln2)
        new_p.append(stp)
        new_s.append(sts)

    def collect(new, batch):
        ca, ha, cb, sre, sim = (jnp.stack([st[j] for st in new], 0) for j in range(5))
        shape_ssm = (DEPTH, batch, SSM_GROUPS, SSM_STATE)
        return (ca.reshape(DEPTH, batch, LRU_CONV - 1, WIDTH), ha, cb.reshape(DEPTH, batch, SC_CONV - 1, WIDTH),
                sre.reshape(shape_ssm), sim.reshape(shape_ssm))

    return ((yp.reshape(x_prompt.shape), ys.reshape(x_sample.shape))
            + collect(new_p, batch_p) + collect(new_s, batch_s))
```

```python
import functools

import jax
import jax.numpy as jnp
from jax import lax
from jax.experimental import pallas as pl
from jax.experimental.pallas import tpu as pltpu

F32 = jnp.float32
BF16 = jnp.bfloat16

D_MODEL = 1024
DEPTH = 2
WIDTH = 512
LRU_HEADS = 8
LRU_HEAD_DIM = WIDTH // LRU_HEADS
LRU_CONV = 4
LRU_C = 8.0
SC_CONV = 3
SSM_GROUP = 16
SSM_GROUPS = WIDTH // SSM_GROUP
SSM_STATE = 64
NS = SSM_GROUPS * SSM_STATE
D_FF = 4 * D_MODEL
N_IN = 6 * WIDTH + 3 * D_MODEL
ALPHA = (2 * DEPTH) ** 0.25
LN_EPS = 1e-5

OFF_XA, OFF_YA, OFF_SB, OFF_SC, OFF_SH, OFF_US, OFF_GL = 0, 512, 1024, 1536, 2048, 2560, 3072

V7X_LANES = 128
V7X_MXU_DIM = 256
V7X_VMEM_LIMIT_BYTES = 58 * 1024 * 1024

ROW_TILE = 512
MIXER_ROW_TILE = 512
GATE_BLK = V7X_MXU_DIM
B_BLK = V7X_LANES
C_BLK = B_BLK // SSM_GROUP * SSM_STATE
S5_SLAB_ELEMS = 4096
FF_BLK = 1024
TAIL_PARTS = 4
TIME_PARTS = 2

ROW_CAW, ROW_CAB, ROW_GXB, ROW_GAB, ROW_LAM, ROW_CBW, ROW_SSD, ROW_GLB = 0, 4, 5, 6, 7, 8, 11, 12


def _dot(a, b):
    return jnp.dot(a, b, preferred_element_type=F32)


def _layer_norm(x, g, b):
    mu = jnp.mean(x, axis=-1, keepdims=True)
    xc = x - mu
    var = jnp.mean(xc * xc, axis=-1, keepdims=True)
    return xc * lax.rsqrt(var + LN_EPS) * g + b


def _load_rows(x_ref, t0=0, t1=None):
    t1 = x_ref.shape[1] // D_MODEL if t1 is None else t1
    return jnp.concatenate([x_ref[:, t * D_MODEL:(t + 1) * D_MODEL] for t in range(t0, t1)], axis=0)


def _store_rows(o_ref, rows, t0=0):
    r = o_ref.shape[0]
    for t in range(rows.shape[0] // r):
        o_ref[:, (t0 + t) * D_MODEL:(t0 + t + 1) * D_MODEL] = rows[t * r:(t + 1) * r, :]


def _s5_prep_kernel(a_re_ref, a_im_ref, log_dt_ref, b_re_ref, b_im_ref,
                    abar_ref, bbar_re_ref, bbar_im_ref):
    a_re = a_re_ref[...]
    a_im = a_im_ref[...]
    step = jnp.exp(log_dt_ref[...])
    mag = jnp.exp(step * a_re)
    abar_re = mag * jnp.cos(step * a_im)
    abar_im = mag * jnp.sin(step * a_im)
    den = a_re * a_re + a_im * a_im
    nr = abar_re - 1.0
    ni = abar_im
    coef_re = (nr * a_re + ni * a_im) / den
    coef_im = (ni * a_re - nr * a_im) / den
    b_re = b_re_ref[...]
    b_im = b_im_ref[...]
    abar_ref[:, 0:1, :] = abar_re
    abar_ref[:, 1:2, :] = abar_im
    bbar_re_ref[...] = coef_re * b_re - coef_im * b_im
    bbar_im_ref[...] = coef_re * b_im + coef_im * b_re


def _s5_prep(ssm_a_re, ssm_a_im, ssm_log_dt, ssm_b_re, ssm_b_im):
    flat = lambda a: a.reshape(DEPTH, 1, NS)
    chan_major = lambda b: jnp.transpose(b, (0, 3, 1, 2)).reshape(DEPTH, SSM_GROUP, NS)
    vec = jax.ShapeDtypeStruct((DEPTH, 2, NS), F32)
    mat = jax.ShapeDtypeStruct((DEPTH, SSM_GROUP, NS), F32)
    return pl.pallas_call(_s5_prep_kernel, out_shape=(vec, mat, mat), name="s5_prep")(
        flat(ssm_a_re), flat(ssm_a_im), flat(ssm_log_dt), chan_major(ssm_b_re), chan_major(ssm_b_im))


def _mixer_kernel(R, has_init, *refs):
    n_state = 5 if has_init else 0
    x_ref = refs[0]
    states_in = refs[1:1 + n_state]
    (w_in, gxw, gaw, pa, pb, pc, gluw, wout, bblk, cre, cim, vw, vln, vab) = refs[1 + n_state:15 + n_state]
    x1_ref, ca_out, ha_out, cb_out, sre_out, sim_out = refs[15 + n_state:21 + n_state]
    xa_ext, u_ext, us_buf, bu, ha, sre, sim = refs[21 + n_state:]
    T = MIXER_ROW_TILE // R
    TR = MIXER_ROW_TILE
    KA = (LRU_CONV - 1) * R
    KB = (SC_CONV - 1) * R
    i = pl.program_id(0)

    @pl.when(i == 0)
    def _():
        if has_init:
            ca_in, ha_in, cb_in, sre_in, sim_in = states_in
            for k in range(LRU_CONV - 1):
                xa_ext[k * R:(k + 1) * R, :] = ca_in[:, k * WIDTH:(k + 1) * WIDTH]
            for k in range(SC_CONV - 1):
                u_ext[k * R:(k + 1) * R, :] = cb_in[:, k * WIDTH:(k + 1) * WIDTH]
            ha[...] = ha_in[...]
            sre[...] = sre_in[...]
            sim[...] = sim_in[...]
        else:
            xa_ext[0:KA, :] = jnp.zeros((KA, WIDTH), F32)
            u_ext[0:KB, :] = jnp.zeros((KB, WIDTH), F32)
            ha[...] = jnp.zeros((R, WIDTH), F32)
            sre[...] = jnp.zeros((R, NS), F32)
            sim[...] = jnp.zeros((R, NS), F32)

    xb = _load_rows(x_ref).astype(BF16)

    def proj(lo, hi):
        return _dot(xb, w_in[:, lo:hi])

    def vrow(r, sl=slice(None)):
        return vw[r:r + 1, sl]

    parts = [(p * T // TIME_PARTS, (p + 1) * T // TIME_PARTS) for p in range(TIME_PARTS)]

    def taps(ext, w_row, n_taps, t0, t1):
        acc = ext[t0 * R:t1 * R, :] * vrow(w_row)
        for k in range(1, n_taps):
            acc = acc + ext[(t0 + k) * R:(t1 + k) * R, :] * vrow(w_row + k)
        return acc

    us_buf[...] = proj(OFF_US, OFF_GL)
    za = []
    for m in range(WIDTH // B_BLK):
        usb = us_buf[:, m * B_BLK:(m + 1) * B_BLK].astype(BF16)
        bum = _dot(usb, bblk[m])
        bu[:, m * C_BLK:(m + 1) * C_BLK] = bum[:, :C_BLK]
        bu[:, NS + m * C_BLK:NS + (m + 1) * C_BLK] = bum[:, C_BLK:]
        za.append(proj(OFF_XA + m * 2 * B_BLK, OFF_XA + (m + 1) * 2 * B_BLK))
    za = jnp.concatenate(za, axis=1)
    xa_ext[KA:KA + TR, :] = za[:, :WIDTH]
    zb = proj(OFF_SB, OFF_US)
    u_ext[KB:KB + TR, :] = zb[:, WIDTH:2 * WIDTH] * zb[:, 2 * WIDTH:]

    softplus_neg_lam = jax.nn.softplus(-vrow(ROW_LAM))
    gate_in = []
    for t0, t1 in parts:
        xa_c = taps(xa_ext, ROW_CAW, LRU_CONV, t0, t1) + vrow(ROW_CAB)
        pre = []
        for hh in range(WIDTH // GATE_BLK):
            xcb = xa_c[:, hh * GATE_BLK:(hh + 1) * GATE_BLK].astype(BF16)
            pre.append((_dot(xcb, gxw[hh]), _dot(xcb, gaw[hh])))
        gate_in.append((xa_c, pre))

    h_lru = ha[...]
    lhs_a = []
    for (t0, t1), (xa_c, pre) in zip(parts, gate_in):
        a_in, b_in = [], []
        for hh in range(WIDTH // GATE_BLK):
            sl = slice(hh * GATE_BLK, (hh + 1) * GATE_BLK)
            gx = jax.nn.sigmoid(pre[hh][0] + vrow(ROW_GXB, sl))
            ga = jax.nn.sigmoid(pre[hh][1] + vrow(ROW_GAB, sl))
            a = jnp.exp(-LRU_C * ga * softplus_neg_lam[:, sl])
            a_in.append(a)
            b_in.append(jnp.sqrt(1.0 - a * a) * (gx * xa_c[:, sl]))
        a_in, b_in = jnp.concatenate(a_in, axis=1), jnp.concatenate(b_in, axis=1)
        h_rows = []
        for t in range(t1 - t0):
            h_lru = a_in[t * R:(t + 1) * R, :] * h_lru + b_in[t * R:(t + 1) * R, :]
            h_rows.append(h_lru)
        h_a = jnp.concatenate(h_rows, axis=0)
        lhs_a.append((h_a * jax.nn.gelu(za[t0 * R:t1 * R, WIDTH:])).astype(BF16))
    ha[...] = h_lru

    lhs_b = [(zb[t0 * R:t1 * R, :WIDTH] * taps(u_ext, ROW_CBW, SC_CONV, t0, t1)).astype(BF16) for t0, t1 in parts]

    slab = max(V7X_LANES, min(C_BLK, S5_SLAB_ELEMS // R))
    slabs = [(slice(lo, lo + slab), slice(NS + lo, NS + lo + slab)) for lo in range(0, NS, slab)]
    abar = [(jnp.broadcast_to(vab[0:1, sl], (R, slab)), jnp.broadcast_to(vab[1:2, sl], (R, slab))) for sl, _ in slabs]
    h_s5 = [(sre[:, sl], sim[:, sl]) for sl, _ in slabs]
    for t0, t1 in parts:
        for s, (sl, sl_im) in enumerate(slabs):
            (ar, ai), (hr, hi) = abar[s], h_s5[s]
            for t in range(t0, t1):
                step = slice(t * R, (t + 1) * R)
                hr, hi = ar * hr - ai * hi + bu[step, sl], ar * hi + ai * hr + bu[step, sl_im]
                bu[step, sl] = hr
                bu[step, sl_im] = hi
            h_s5[s] = (hr, hi)
    for (sl, _), (hr, hi) in zip(slabs, h_s5):
        sre[:, sl] = hr
        sim[:, sl] = hi

    gates, out_ab, zc, out_c = {}, {}, {}, {}

    def stage_gates(p):
        t0, t1 = parts[p]
        xb_p = xb[t0 * R:t1 * R, :]
        gates[p] = [jax.nn.sigmoid(_dot(xb_p, w_in[:, OFF_GL + j * D_MODEL:OFF_GL + (j + 1) * D_MODEL]))
                    for j in range(3)]

    def stage_branches(p):
        t0, t1 = parts[p]
        rows = slice(t0 * R, t1 * R)
        out_ab[p] = (_dot(lhs_a[p], pa[...]), _dot(lhs_b[p], pb[...]))
        y = []
        for k in range(NS // C_BLK):
            hre = bu[rows, k * C_BLK:(k + 1) * C_BLK].astype(BF16)
            him = bu[rows, NS + k * C_BLK:NS + (k + 1) * C_BLK].astype(BF16)
            sl = slice(k * B_BLK, (k + 1) * B_BLK)
            y.append(_dot(hre, cre[k]) + _dot(him, cim[k]) + vrow(ROW_SSD, sl) * us_buf[rows, sl])
        zc[p] = jax.nn.gelu(jnp.concatenate(y, axis=1))

    def stage_glu(p):
        zc[p] = (zc[p] * jax.nn.sigmoid(_dot(zc[p].astype(BF16), gluw[...]) + vrow(ROW_GLB))).astype(BF16)

    def stage_proj_c(p):
        out_c[p] = _dot(zc[p], pc[...])

    def stage_out(p):
        t0, t1 = parts[p]
        m = gates[p][0] * out_ab[p][0] + gates[p][1] * out_ab[p][1] + gates[p][2] * out_c[p]
        out = _dot(m.astype(BF16), wout[...])
        y1 = _layer_norm(ALPHA * _load_rows(x_ref, t0, t1) + out, vln[0:1, :], vln[1:2, :])
        _store_rows(x1_ref, y1, t0)

    for stage in (stage_gates, stage_branches, stage_glu, stage_proj_c, stage_out):
        for p in range(TIME_PARTS):
            stage(p)

    xa_ext[0:KA, :] = xa_ext[TR:TR + KA, :]
    u_ext[0:KB, :] = u_ext[TR:TR + KB, :]

    @pl.when(i == pl.num_programs(0) - 1)
    def _():
        for k in range(LRU_CONV - 1):
            ca_out[:, k * WIDTH:(k + 1) * WIDTH] = xa_ext[k * R:(k + 1) * R, :]
        for k in range(SC_CONV - 1):
            cb_out[:, k * WIDTH:(k + 1) * WIDTH] = u_ext[k * R:(k + 1) * R, :]
        ha_out[...] = ha[...]
        sre_out[...] = sre[...]
        sim_out[...] = sim[...]


def _layer_block(layer, shape):
    zeros = (0,) * (len(shape) - 1)
    return pl.BlockSpec((None,) + tuple(shape[1:]), lambda i: (layer,) + zeros, pipeline_mode=pl.Buffered(1))


def _mixer(x, layer, states, wts):
    R = x.shape[0]
    tile = MIXER_ROW_TILE
    T = tile // R
    n_steps = x.shape[1] // (T * D_MODEL)
    assert tile % R == 0 and x.shape[1] % (T * D_MODEL) == 0
    x_spec = pl.BlockSpec((R, T * D_MODEL), lambda i: (0, i))
    state_shapes = [(R, (LRU_CONV - 1) * WIDTH), (R, WIDTH), (R, (SC_CONV - 1) * WIDTH), (R, NS), (R, NS)]
    states = () if states is None else tuple(states)
    in_specs = ([x_spec] + [_layer_block(layer, s.shape) for s in states]
                + [_layer_block(layer, w.shape) for w in wts])
    out_shape = [jax.ShapeDtypeStruct(x.shape, F32)] + [jax.ShapeDtypeStruct(s, F32) for s in state_shapes]
    out_specs = [x_spec] + [pl.BlockSpec(s, lambda i: (0, 0)) for s in state_shapes]
    scratch = [
        pltpu.VMEM(((LRU_CONV - 1) * R + tile, WIDTH), F32),
        pltpu.VMEM(((SC_CONV - 1) * R + tile, WIDTH), F32),
        pltpu.VMEM((tile, WIDTH), F32),
        pltpu.VMEM((tile, 2 * NS), F32),
        pltpu.VMEM((R, WIDTH), F32),
        pltpu.VMEM((R, NS), F32),
        pltpu.VMEM((R, NS), F32),
    ]
    outs = pl.pallas_call(
        functools.partial(_mixer_kernel, R, bool(states)),
        grid=(n_steps,), in_specs=in_specs, out_specs=out_specs, out_shape=out_shape,
        scratch_shapes=scratch,
        compiler_params=pltpu.CompilerParams(dimension_semantics=("arbitrary",),
                                             vmem_limit_bytes=V7X_VMEM_LIMIT_BYTES),
        name=f"mixer_r{R}",
    )(x, *states, *wts)
    return outs[0], outs[1:]


def _mlp_block(x_ref, o_ref, up, down, vln):
    x = _load_rows(x_ref)
    xb = x.astype(BF16)
    acc = ALPHA * x
    n_ff = D_FF // FF_BLK
    for c in range(n_ff):
        h = _dot(xb, up[:, c * FF_BLK:(c + 1) * FF_BLK].astype(BF16))
        h = jnp.square(jnp.maximum(h, 0.0)).astype(BF16)
        if c < n_ff - 1:
            acc = acc + _dot(h, down[c * FF_BLK:(c + 1) * FF_BLK, :].astype(BF16))
    r = x_ref.shape[0]
    steps = x.shape[0] // r
    down_last = down[(n_ff - 1) * FF_BLK:, :].astype(BF16)
    for part in range(TAIL_PARTS):
        t0, t1 = part * steps // TAIL_PARTS, (part + 1) * steps // TAIL_PARTS
        rows = slice(t0 * r, t1 * r)
        f = acc[rows, :] + _dot(h[rows, :], down_last)
        _store_rows(o_ref, _layer_norm(f, vln[0:1, :], vln[1:2, :]), t0)


def _mlp_kernel(n_prompt, xp_ref, xs_ref, up, down, vln, op_ref, os_ref):
    i = pl.program_id(0)

    @pl.when(i < n_prompt)
    def _():
        _mlp_block(xp_ref, op_ref, up, down, vln)

    @pl.when(i == n_prompt)
    def _():
        _mlp_block(xs_ref, os_ref, up, down, vln)


def _mlp(xp, xs, layer, up, down, vln):
    rp, rs = xp.shape[0], xs.shape[0]
    tp = ROW_TILE // rp
    n_prompt = xp.shape[1] // (tp * D_MODEL)
    assert xs.shape[1] * rs == ROW_TILE * D_MODEL
    p_spec = pl.BlockSpec((rp, tp * D_MODEL), lambda i: (0, jnp.minimum(i, n_prompt - 1)))
    s_spec = pl.BlockSpec(xs.shape, lambda i: (0, 0))
    return pl.pallas_call(
        functools.partial(_mlp_kernel, n_prompt),
        grid=(n_prompt + 1,),
        in_specs=[p_spec, s_spec] + [_layer_block(layer, w.shape) for w in (up, down, vln)],
        out_specs=[p_spec, s_spec],
        out_shape=[jax.ShapeDtypeStruct(xp.shape, F32), jax.ShapeDtypeStruct(xs.shape, F32)],
        compiler_params=pltpu.CompilerParams(dimension_semantics=("arbitrary",),
                                             vmem_limit_bytes=V7X_VMEM_LIMIT_BYTES),
        name="mlp",
    )(xp, xs, up, down, vln)


def _block_diag(blocks):
    n, a, b = blocks.shape[-3:]
    eye = jnp.eye(n, dtype=blocks.dtype)
    out = jnp.einsum("...nab,nm->...namb", blocks, eye)
    return out.reshape(*blocks.shape[:-3], n * a, n * b)


def kernel(x_prompt, x_sample, state_lru_conv, state_lru_h, state_sconv, state_ssm_re, state_ssm_im,
           w_in, conv_a_w, conv_a_b, gate_x_w, gate_x_b, gate_a_w, gate_a_b, lru_lambda, conv_b_w,
           ssm_a_re, ssm_a_im, ssm_log_dt, ssm_b_re, ssm_b_im, ssm_c_re, ssm_c_im, ssm_d, glu_w, glu_b,
           proj_a, proj_b, proj_c, w_out, ln1_g, ln1_b, mlp_up, mlp_down, ln2_g, ln2_b):
    batch_p, batch_s = x_prompt.shape[0], x_sample.shape[0]
    bf = lambda w: w.astype(BF16)

    vab, bbar_re, bbar_im = _s5_prep(ssm_a_re, ssm_a_im, ssm_log_dt, ssm_b_re, ssm_b_im)

    def b_blocks(bbar):
        bb = bbar.reshape(DEPTH, SSM_GROUP, WIDTH // B_BLK, B_BLK // SSM_GROUP, SSM_STATE)
        return _block_diag(jnp.transpose(bb, (0, 2, 3, 1, 4)))

    bblk = bf(jnp.concatenate([b_blocks(bbar_re), b_blocks(bbar_im)], axis=-1))

    def c_blocks(c):
        cc = c.reshape(DEPTH, NS // C_BLK, C_BLK // SSM_STATE, SSM_GROUP, SSM_STATE)
        return bf(_block_diag(jnp.swapaxes(cc, -1, -2)))

    cre, cim = c_blocks(ssm_c_re), c_blocks(-ssm_c_im)

    def gate_blocks(w):
        heads = GATE_BLK // LRU_HEAD_DIM
        return bf(_block_diag(w.reshape(DEPTH, LRU_HEADS // heads, heads, LRU_HEAD_DIM, LRU_HEAD_DIM)))

    gxw, gaw = gate_blocks(gate_x_w), gate_blocks(gate_a_w)

    row = lambda v: v[:, None, :]
    vw = jnp.concatenate([conv_a_w, row(conv_a_b), row(gate_x_b), row(gate_a_b), row(lru_lambda),
                          conv_b_w, row(ssm_d), row(glu_b), jnp.zeros((DEPTH, 3, WIDTH), F32)], axis=1)
    vln1 = jnp.stack([ln1_g, ln1_b], axis=1)
    vln2 = jnp.stack([ln2_g, ln2_b], axis=1)
    wts = (bf(w_in), gxw, gaw, bf(proj_a), bf(proj_b), bf(proj_c), bf(glu_w), bf(w_out), bblk, cre, cim,
           vw, vln1, vab)
    up, down = mlp_up, mlp_down

    states_s = (state_lru_conv.reshape(DEPTH, batch_s, (LRU_CONV - 1) * WIDTH), state_lru_h,
                state_sconv.reshape(DEPTH, batch_s, (SC_CONV - 1) * WIDTH),
                state_ssm_re.reshape(DEPTH, batch_s, NS), state_ssm_im.reshape(DEPTH, batch_s, NS))

    yp = x_prompt.reshape(batch_p, -1)
    ys = x_sample.reshape(batch_s, -1)
    new_p, new_s = [], []
    for layer in range(DEPTH):
        yp, stp = _mixer(yp, layer, None, wts)
        ys, sts = _mixer(ys, layer, states_s, wts)
        yp, ys = _mlp(yp, ys, layer, up, down, vln2)
        new_p.append(stp)
        new_s.append(sts)

    def collect(new, batch):
        ca, ha, cb, sre, sim = (jnp.stack([st[j] for st in new], 0) for j in range(5))
        shape_ssm = (DEPTH, batch, SSM_GROUPS, SSM_STATE)
        return (ca.reshape(DEPTH, batch, LRU_CONV - 1, WIDTH), ha, cb.reshape(DEPTH, batch, SC_CONV - 1, WIDTH),
                sre.reshape(shape_ssm), sim.reshape(shape_ssm))

    return ((yp.reshape(x_prompt.shape), ys.reshape(x_sample.shape))
            + collect(new_p, batch_p) + collect(new_s, batch_s))
```

```python
import functools

import jax
import jax.numpy as jnp
from jax import lax
from jax.experimental import pallas as pl
from jax.experimental.pallas import tpu as pltpu

F32 = jnp.float32
BF16 = jnp.bfloat16

D_MODEL = 1024
DEPTH = 2
WIDTH = 512
LRU_HEADS = 8
LRU_HEAD_DIM = WIDTH // LRU_HEADS
LRU_CONV = 4
LRU_C = 8.0
SC_CONV = 3
SSM_GROUP = 16
SSM_GROUPS = WIDTH // SSM_GROUP
SSM_STATE = 64
NS = SSM_GROUPS * SSM_STATE
D_FF = 4 * D_MODEL
N_IN = 6 * WIDTH + 3 * D_MODEL
ALPHA = (2 * DEPTH) ** 0.25
LN_EPS = 1e-5

OFF_XA, OFF_YA, OFF_SB, OFF_SC, OFF_SH, OFF_US, OFF_GL = 0, 512, 1024, 1536, 2048, 2560, 3072

V7X_LANES = 128
V7X_MXU_DIM = 256
V7X_VMEM_LIMIT_BYTES = 58 * 1024 * 1024

ROW_TILE = 512
MIXER_ROW_TILE = 512
GATE_BLK = V7X_MXU_DIM
B_BLK = V7X_LANES
C_BLK = B_BLK // SSM_GROUP * SSM_STATE
S5_SLAB_ELEMS = 4096
FF_BLK = 1024
TAIL_PARTS = 4
TIME_PARTS = 2

ROW_CAW, ROW_CAB, ROW_GXB, ROW_GAB, ROW_LAM, ROW_CBW, ROW_SSD, ROW_GLB = 0, 4, 5, 6, 7, 8, 11, 12


def _dot(a, b):
    return jnp.dot(a, b, preferred_element_type=F32)


def _layer_norm(x, g, b):
    mu = jnp.mean(x, axis=-1, keepdims=True)
    xc = x - mu
    var = jnp.mean(xc * xc, axis=-1, keepdims=True)
    return xc * lax.rsqrt(var + LN_EPS) * g + b


def _load_rows(x_ref, t0=0, t1=None):
    t1 = x_ref.shape[1] // D_MODEL if t1 is None else t1
    return jnp.concatenate([x_ref[:, t * D_MODEL:(t + 1) * D_MODEL] for t in range(t0, t1)], axis=0)


def _store_rows(o_ref, rows, t0=0):
    r = o_ref.shape[0]
    for t in range(rows.shape[0] // r):
        o_ref[:, (t0 + t) * D_MODEL:(t0 + t + 1) * D_MODEL] = rows[t * r:(t + 1) * r, :]


def _s5_prep_kernel(a_re_ref, a_im_ref, log_dt_ref, b_re_ref, b_im_ref,
                    abar_ref, bbar_re_ref, bbar_im_ref):
    a_re = a_re_ref[...]
    a_im = a_im_ref[...]
    step = jnp.exp(log_dt_ref[...])
    mag = jnp.exp(step * a_re)
    abar_re = mag * jnp.cos(step * a_im)
    abar_im = mag * jnp.sin(step * a_im)
    den = a_re * a_re + a_im * a_im
    nr = abar_re - 1.0
    ni = abar_im
    coef_re = (nr * a_re + ni * a_im) / den
    coef_im = (ni * a_re - nr * a_im) / den
    b_re = b_re_ref[...]
    b_im = b_im_ref[...]
    abar_ref[:, 0:1, :] = abar_re
    abar_ref[:, 1:2, :] = abar_im
    bbar_re_ref[...] = coef_re * b_re - coef_im * b_im
    bbar_im_ref[...] = coef_re * b_im + coef_im * b_re


def _s5_prep(ssm_a_re, ssm_a_im, ssm_log_dt, ssm_b_re, ssm_b_im):
    flat = lambda a: a.reshape(DEPTH, 1, NS)
    chan_major = lambda b: jnp.transpose(b, (0, 3, 1, 2)).reshape(DEPTH, SSM_GROUP, NS)
    vec = jax.ShapeDtypeStruct((DEPTH, 2, NS), F32)
    mat = jax.ShapeDtypeStruct((DEPTH, SSM_GROUP, NS), F32)
    return pl.pallas_call(_s5_prep_kernel, out_shape=(vec, mat, mat), name="s5_prep")(
        flat(ssm_a_re), flat(ssm_a_im), flat(ssm_log_dt), chan_major(ssm_b_re), chan_major(ssm_b_im))


def _mixer_kernel(R, has_init, *refs):
    n_state = 5 if has_init else 0
    x_ref = refs[0]
    states_in = refs[1:1 + n_state]
    (w_in, gxw, gaw, pa, pb, pc, gluw, wout, bblk, cre, cim, vw, vln, vab) = refs[1 + n_state:15 + n_state]
    x1_ref, ca_out, ha_out, cb_out, sre_out, sim_out = refs[15 + n_state:21 + n_state]
    xa_ext, u_ext, us_buf, bu, ha, sre, sim = refs[21 + n_state:]
    T = MIXER_ROW_TILE // R
    TR = MIXER_ROW_TILE
    KA = (LRU_CONV - 1) * R
    KB = (SC_CONV - 1) * R
    i = pl.program_id(0)

    @pl.when(i == 0)
    def _():
        if has_init:
            ca_in, ha_in, cb_in, sre_in, sim_in = states_in
            for k in range(LRU_CONV - 1):
                xa_ext[k * R:(k + 1) * R, :] = ca_in[:, k * WIDTH:(k + 1) * WIDTH]
            for k in range(SC_CONV - 1):
                u_ext[k * R:(k + 1) * R, :] = cb_in[:, k * WIDTH:(k + 1) * WIDTH]
            ha[...] = ha_in[...]
            sre[...] = sre_in[...]
            sim[...] = sim_in[...]
        else:
            xa_ext[0:KA, :] = jnp.zeros((KA, WIDTH), F32)
            u_ext[0:KB, :] = jnp.zeros((KB, WIDTH), F32)
            ha[...] = jnp.zeros((R, WIDTH), F32)
            sre[...] = jnp.zeros((R, NS), F32)
            sim[...] = jnp.zeros((R, NS), F32)

    xb = _load_rows(x_ref).astype(BF16)

    def proj(lo, hi):
        return _dot(xb, w_in[:, lo:hi])

    def vrow(r, sl=slice(None)):
        return vw[r:r + 1, sl]

    parts = [(p * T // TIME_PARTS, (p + 1) * T // TIME_PARTS) for p in range(TIME_PARTS)]

    def taps(ext, w_row, n_taps, t0, t1):
        acc = ext[t0 * R:t1 * R, :] * vrow(w_row)
        for k in range(1, n_taps):
            acc = acc + ext[(t0 + k) * R:(t1 + k) * R, :] * vrow(w_row + k)
        return acc

    us_buf[...] = proj(OFF_US, OFF_GL)
    za = []
    for m in range(WIDTH // B_BLK):
        usb = us_buf[:, m * B_BLK:(m + 1) * B_BLK].astype(BF16)
        bum = _dot(usb, bblk[m])
        bu[:, m * C_BLK:(m + 1) * C_BLK] = bum[:, :C_BLK]
        bu[:, NS + m * C_BLK:NS + (m + 1) * C_BLK] = bum[:, C_BLK:]
        za.append(proj(OFF_XA + m * 2 * B_BLK, OFF_XA + (m + 1) * 2 * B_BLK))
    za = jnp.concatenate(za, axis=1)
    xa_ext[KA:KA + TR, :] = za[:, :WIDTH]
    zb = proj(OFF_SB, OFF_US)
    u_ext[KB:KB + TR, :] = zb[:, WIDTH:2 * WIDTH] * zb[:, 2 * WIDTH:]

    softplus_neg_lam = jax.nn.softplus(-vrow(ROW_LAM))
    gate_in = []
    for t0, t1 in parts:
        xa_c = taps(xa_ext, ROW_CAW, LRU_CONV, t0, t1) + vrow(ROW_CAB)
        pre = []
        for hh in range(WIDTH // GATE_BLK):
            xcb = xa_c[:, hh * GATE_BLK:(hh + 1) * GATE_BLK].astype(BF16)
            pre.append((_dot(xcb, gxw[hh]), _dot(xcb, gaw[hh])))
        gate_in.append((xa_c, pre))

    h_lru = ha[...]
    lhs_a = []
    for (t0, t1), (xa_c, pre) in zip(parts, gate_in):
        a_in, b_in = [], []
        for hh in range(WIDTH // GATE_BLK):
            sl = slice(hh * GATE_BLK, (hh + 1) * GATE_BLK)
            gx = jax.nn.sigmoid(pre[hh][0] + vrow(ROW_GXB, sl))
            ga = jax.nn.sigmoid(pre[hh][1] + vrow(ROW_GAB, sl))
            a = jnp.exp(-LRU_C * ga * softplus_neg_lam[:, sl])
            a_in.append(a)
            b_in.append(jnp.sqrt(1.0 - a * a) * (gx * xa_c[:, sl]))
        a_in, b_in = jnp.concatenate(a_in, axis=1), jnp.concatenate(b_in, axis=1)
        h_rows = []
        for t in range(t1 - t0):
            h_lru = a_in[t * R:(t + 1) * R, :] * h_lru + b_in[t * R:(t + 1) * R, :]
            h_rows.append(h_lru)
        h_a = jnp.concatenate(h_rows, axis=0)
        lhs_a.append((h_a * jax.nn.gelu(za[t0 * R:t1 * R, WIDTH:])).astype(BF16))
    ha[...] = h_lru

    lhs_b = [(zb[t0 * R:t1 * R, :WIDTH] * taps(u_ext, ROW_CBW, SC_CONV, t0, t1)).astype(BF16) for t0, t1 in parts]

    slab = max(V7X_LANES, min(C_BLK, S5_SLAB_ELEMS // R))
    slabs = [(slice(lo, lo + slab), slice(NS + lo, NS + lo + slab)) for lo in range(0, NS, slab)]
    abar = [(jnp.broadcast_to(vab[0:1, sl], (R, slab)), jnp.broadcast_to(vab[1:2, sl], (R, slab))) for sl, _ in slabs]
    h_s5 = [(sre[:, sl], sim[:, sl]) for sl, _ in slabs]
    for t0, t1 in parts:
        for s, (sl, sl_im) in enumerate(slabs):
            (ar, ai), (hr, hi) = abar[s], h_s5[s]
            for t in range(t0, t1):
                step = slice(t * R, (t + 1) * R)
                hr, hi = ar * hr - ai * hi + bu[step, sl], ar * hi + ai * hr + bu[step, sl_im]
                bu[step, sl] = hr
                bu[step, sl_im] = hi
            h_s5[s] = (hr, hi)
    for (sl, _), (hr, hi) in zip(slabs, h_s5):
        sre[:, sl] = hr
        sim[:, sl] = hi

    gates, out_ab, zc, out_c = {}, {}, {}, {}

    def stage_gates(p):
        t0, t1 = parts[p]
        xb_p = xb[t0 * R:t1 * R, :]
        gates[p] = [jax.nn.sigmoid(_dot(xb_p, w_in[:, OFF_GL + j * D_MODEL:OFF_GL + (j + 1) * D_MODEL]))
                    for j in range(3)]

    def stage_branches(p):
        t0, t1 = parts[p]
        rows = slice(t0 * R, t1 * R)
        out_ab[p] = (_dot(lhs_a[p], pa[...]), _dot(lhs_b[p], pb[...]))
        y = []
        for k in range(NS // C_BLK):
            hre = bu[rows, k * C_BLK:(k + 1) * C_BLK].astype(BF16)
            him = bu[rows, NS + k * C_BLK:NS + (k + 1) * C_BLK].astype(BF16)
            sl = slice(k * B_BLK, (k + 1) * B_BLK)
            y.append(_dot(hre, cre[k]) + _dot(him, cim[k]) + vrow(ROW_SSD, sl) * us_buf[rows, sl])
        zc[p] = jax.nn.gelu(jnp.concatenate(y, axis=1))

    def stage_glu(p):
        zc[p] = (zc[p] * jax.nn.sigmoid(_dot(zc[p].astype(BF16), gluw[...]) + vrow(ROW_GLB))).astype(BF16)

    def stage_proj_c(p):
        out_c[p] = _dot(zc[p], pc[...])

    def stage_out(p):
        t0, t1 = parts[p]
        m = gates[p][0] * out_ab[p][0] + gates[p][1] * out_ab[p][1] + gates[p][2] * out_c[p]
        out = _dot(m.astype(BF16), wout[...])
        y1 = _layer_norm(ALPHA * _load_rows(x_ref, t0, t1) + out, vln[0:1, :], vln[1:2, :])
        _store_rows(x1_ref, y1, t0)

    for stage in (stage_gates, stage_branches, stage_glu, stage_proj_c, stage_out):
        for p in range(TIME_PARTS):
            stage(p)

    xa_ext[0:KA, :] = xa_ext[TR:TR + KA, :]
    u_ext[0:KB, :] = u_ext[TR:TR + KB, :]

    @pl.when(i == pl.num_programs(0) - 1)
    def _():
        for k in range(LRU_CONV - 1):
            ca_out[:, k * WIDTH:(k + 1) * WIDTH] = xa_ext[k * R:(k + 1) * R, :]
        for k in range(SC_CONV - 1):
            cb_out[:, k * WIDTH:(k + 1) * WIDTH] = u_ext[k * R:(k + 1) * R, :]
        ha_out[...] = ha[...]
        sre_out[...] = sre[...]
        sim_out[...] = sim[...]


def _layer_block(layer, shape):
    zeros = (0,) * (len(shape) - 1)
    return pl.BlockSpec((None,) + tuple(shape[1:]), lambda i: (layer,) + zeros, pipeline_mode=pl.Buffered(1))


def _mixer(x, layer, states, wts):
    R = x.shape[0]
    tile = MIXER_ROW_TILE
    T = tile // R
    n_steps = x.shape[1] // (T * D_MODEL)
    assert tile % R == 0 and x.shape[1] % (T * D_MODEL) == 0
    x_spec = pl.BlockSpec((R, T * D_MODEL), lambda i: (0, i))
    state_shapes = [(R, (LRU_CONV - 1) * WIDTH), (R, WIDTH), (R, (SC_CONV - 1) * WIDTH), (R, NS), (R, NS)]
    states = () if states is None else tuple(states)
    in_specs = ([x_spec] + [_layer_block(layer, s.shape) for s in states]
                + [_layer_block(layer, w.shape) for w in wts])
    out_shape = [jax.ShapeDtypeStruct(x.shape, F32)] + [jax.ShapeDtypeStruct(s, F32) for s in state_shapes]
    out_specs = [x_spec] + [pl.BlockSpec(s, lambda i: (0, 0)) for s in state_shapes]
    scratch = [
        pltpu.VMEM(((LRU_CONV - 1) * R + tile, WIDTH), F32),
        pltpu.VMEM(((SC_CONV - 1) * R + tile, WIDTH), F32),
        pltpu.VMEM((tile, WIDTH), F32),
        pltpu.VMEM((tile, 2 * NS), F32),
        pltpu.VMEM((R, WIDTH), F32),
        pltpu.VMEM((R, NS), F32),
        pltpu.VMEM((R, NS), F32),
    ]
    outs = pl.pallas_call(
        functools.partial(_mixer_kernel, R, bool(states)),
        grid=(n_steps,), in_specs=in_specs, out_specs=out_specs, out_shape=out_shape,
        scratch_shapes=scratch,
        compiler_params=pltpu.CompilerParams(dimension_semantics=("arbitrary",),
                                             vmem_limit_bytes=V7X_VMEM_LIMIT_BYTES),
        name=f"mixer_r{R}",
    )(x, *states, *wts)
    return outs[0], outs[1:]


def _mlp_block(x_ref, o_ref, up, down, vln):
    x = _load_rows(x_ref)
    xb = x.astype(BF16)
    acc = ALPHA * x
    n_ff = D_FF // FF_BLK
    for c in range(n_ff):
        h = _dot(xb, up[:, c * FF_BLK:(c + 1) * FF_BLK].astype(BF16))
        h = jnp.square(jnp.maximum(h, 0.0)).astype(BF16)
        if c < n_ff - 1:
            acc = acc + _dot(h, down[c * FF_BLK:(c + 1) * FF_BLK, :].astype(BF16))
    r = x_ref.shape[0]
    steps = x.shape[0] // r
    down_last = down[(n_ff - 1) * FF_BLK:, :].astype(BF16)
    for part in range(TAIL_PARTS):
        t0, t1 = part * steps // TAIL_PARTS, (part + 1) * steps // TAIL_PARTS
        rows = slice(t0 * r, t1 * r)
        f = acc[rows, :] + _dot(h[rows, :], down_last)
        _store_rows(o_ref, _layer_norm(f, vln[0:1, :], vln[1:2, :]), t0)


def _mlp_kernel(n_prompt, xp_ref, xs_ref, up, down, vln, op_ref, os_ref):
    i = pl.program_id(0)

    @pl.when(i < n_prompt)
    def _():
        _mlp_block(xp_ref, op_ref, up, down, vln)

    @pl.when(i == n_prompt)
    def _():
        _mlp_block(xs_ref, os_ref, up, down, vln)


def _mlp(xp, xs, layer, up, down, vln):
    rp, rs = xp.shape[0], xs.shape[0]
    tp = ROW_TILE // rp
    n_prompt = xp.shape[1] // (tp * D_MODEL)
    assert xs.shape[1] * rs == ROW_TILE * D_MODEL
    p_spec = pl.BlockSpec((rp, tp * D_MODEL), lambda i: (0, jnp.minimum(i, n_prompt - 1)))
    s_spec = pl.BlockSpec(xs.shape, lambda i: (0, 0))
    return pl.pallas_call(
        functools.partial(_mlp_kernel, n_prompt),
        grid=(n_prompt + 1,),
        in_specs=[p_spec, s_spec] + [_layer_block(layer, w.shape) for w in (up, down, vln)],
        out_specs=[p_spec, s_spec],
        out_shape=[jax.ShapeDtypeStruct(xp.shape, F32), jax.ShapeDtypeStruct(xs.shape, F32)],
        compiler_params=pltpu.CompilerParams(dimension_semantics=("arbitrary",),
                                             vmem_limit_bytes=V7X_VMEM_LIMIT_BYTES),
        name="mlp",
    )(xp, xs, up, down, vln)


def _block_diag(blocks):
    n, a, b = blocks.shape[-3:]
    eye = jnp.eye(n, dtype=blocks.dtype)
    out = jnp.einsum("...nab,nm->...namb", blocks, eye)
    return out.reshape(*blocks.shape[:-3], n * a, n * b)


def kernel(x_prompt, x_sample, state_lru_conv, state_lru_h, state_sconv, state_ssm_re, state_ssm_im,
           w_in, conv_a_w, conv_a_b, gate_x_w, gate_x_b, gate_a_w, gate_a_b, lru_lambda, conv_b_w,
           ssm_a_re, ssm_a_im, ssm_log_dt, ssm_b_re, ssm_b_im, ssm_c_re, ssm_c_im, ssm_d, glu_w, glu_b,
           proj_a, proj_b, proj_c, w_out, ln1_g, ln1_b, mlp_up, mlp_down, ln2_g, ln2_b):
    batch_p, batch_s = x_prompt.shape[0], x_sample.shape[0]
    bf = lambda w: w.astype(BF16)

    vab, bbar_re, bbar_im = _s5_prep(ssm_a_re, ssm_a_im, ssm_log_dt, ssm_b_re, ssm_b_im)

    def b_blocks(bbar):
        bb = bbar.reshape(DEPTH, SSM_GROUP, WIDTH // B_BLK, B_BLK // SSM_GROUP, SSM_STATE)
        return _block_diag(jnp.transpose(bb, (0, 2, 3, 1, 4)))

    bblk = bf(jnp.concatenate([b_blocks(bbar_re), b_blocks(bbar_im)], axis=-1))

    def c_blocks(c):
        cc = c.reshape(DEPTH, NS // C_BLK, C_BLK // SSM_STATE, SSM_GROUP, SSM_STATE)
        return bf(_block_diag(jnp.swapaxes(cc, -1, -2)))

    cre, cim = c_blocks(ssm_c_re), c_blocks(-ssm_c_im)

    def gate_blocks(w):
        heads = GATE_BLK // LRU_HEAD_DIM
        return bf(_block_diag(w.reshape(DEPTH, LRU_HEADS // heads, heads, LRU_HEAD_DIM, LRU_HEAD_DIM)))

    gxw, gaw = gate_blocks(gate_x_w), gate_blocks(gate_a_w)

    row = lambda v: v[:, None, :]
    vw = jnp.concatenate([conv_a_w, row(conv_a_b), row(gate_x_b), row(gate_a_b), row(lru_lambda),
                          conv_b_w, row(ssm_d), row(glu_b), jnp.zeros((DEPTH, 3, WIDTH), F32)], axis=1)
    vln1 = jnp.stack([ln1_g, ln1_b], axis=1)
    vln2 = jnp.stack([ln2_g, ln2_b], axis=1)
    wts = (bf(w_in), gxw, gaw, bf(proj_a), bf(proj_b), bf(proj_c), bf(glu_w), bf(w_out), bblk, cre, cim,
           vw, vln1, vab)
    up, down = mlp_up, mlp_down

    states_s = (state_lru_conv.reshape(DEPTH, batch_s, (LRU_CONV - 1) * WIDTH), state_lru_h,
                state_sconv.reshape(DEPTH, batch_s, (SC_CONV - 1) * WIDTH),
                state_ssm_re.reshape(DEPTH, batch_s, NS), state_ssm_im.reshape(DEPTH, batch_s, NS))

    yp = x_prompt.reshape(batch_p, -1)
    ys = x_sample.reshape(batch_s, -1)
    new_p, new_s = [], []
    for layer in range(DEPTH):
        ys, sts = _mixer(ys, layer, states_s, wts)
        yp, stp = _mixer(yp, layer, None, wts)
        yp, ys = _mlp(yp, ys, layer, up, down, vln2)
        new_p.append(stp)
        new_s.append(sts)

    def collect(new, batch):
        ca, ha, cb, sre, sim = (jnp.stack([st[j] for st in new], 0) for j in range(5))
        shape_ssm = (DEPTH, batch, SSM_GROUPS, SSM_STATE)
        return (ca.reshape(DEPTH, batch, LRU_CONV - 1, WIDTH), ha, cb.reshape(DEPTH, batch, SC_CONV - 1, WIDTH),
                sre.reshape(shape_ssm), sim.reshape(shape_ssm))

    return ((yp.reshape(x_prompt.shape), ys.reshape(x_sample.shape))
            + collect(new_p, batch_p) + collect(new_s, batch_s))
```

```python
import functools

import jax
import jax.numpy as jnp
from jax import lax
from jax.experimental import pallas as pl
from jax.experimental.pallas import tpu as pltpu

F32 = jnp.float32
BF16 = jnp.bfloat16

D_MODEL = 1024
DEPTH = 2
WIDTH = 512
LRU_HEADS = 8
LRU_HEAD_DIM = WIDTH // LRU_HEADS
LRU_CONV = 4
LRU_C = 8.0
SC_CONV = 3
SSM_GROUP = 16
SSM_GROUPS = WIDTH // SSM_GROUP
SSM_STATE = 64
NS = SSM_GROUPS * SSM_STATE
D_FF = 4 * D_MODEL
N_IN = 6 * WIDTH + 3 * D_MODEL
ALPHA = (2 * DEPTH) ** 0.25
LN_EPS = 1e-5

OFF_XA, OFF_YA, OFF_SB, OFF_SC, OFF_SH, OFF_US, OFF_GL = 0, 512, 1024, 1536, 2048, 2560, 3072

V7X_LANES = 128
V7X_MXU_DIM = 256
V7X_VMEM_LIMIT_BYTES = 58 * 1024 * 1024

ROW_TILE = 1024
MLP_SUB_ROWS = 512
MIXER_ROW_TILE = 512
GATE_BLK = V7X_MXU_DIM
B_BLK = V7X_LANES
C_BLK = B_BLK // SSM_GROUP * SSM_STATE
S5_SLAB_ELEMS = 4096
FF_BLK = 1024
TAIL_PARTS = 4
TIME_PARTS = 2

ROW_CAW, ROW_CAB, ROW_GXB, ROW_GAB, ROW_LAM, ROW_CBW, ROW_SSD, ROW_GLB = 0, 4, 5, 6, 7, 8, 11, 12


def _dot(a, b):
    return jnp.dot(a, b, preferred_element_type=F32)


def _layer_norm(x, g, b):
    mu = jnp.mean(x, axis=-1, keepdims=True)
    xc = x - mu
    var = jnp.mean(xc * xc, axis=-1, keepdims=True)
    return xc * lax.rsqrt(var + LN_EPS) * g + b


def _load_rows(x_ref, t0=0, t1=None):
    t1 = x_ref.shape[1] // D_MODEL if t1 is None else t1
    return jnp.concatenate([x_ref[:, t * D_MODEL:(t + 1) * D_MODEL] for t in range(t0, t1)], axis=0)


def _store_rows(o_ref, rows, t0=0):
    r = o_ref.shape[0]
    for t in range(rows.shape[0] // r):
        o_ref[:, (t0 + t) * D_MODEL:(t0 + t + 1) * D_MODEL] = rows[t * r:(t + 1) * r, :]


def _s5_prep_kernel(a_re_ref, a_im_ref, log_dt_ref, b_re_ref, b_im_ref,
                    abar_ref, bbar_re_ref, bbar_im_ref):
    a_re = a_re_ref[...]
    a_im = a_im_ref[...]
    step = jnp.exp(log_dt_ref[...])
    mag = jnp.exp(step * a_re)
    abar_re = mag * jnp.cos(step * a_im)
    abar_im = mag * jnp.sin(step * a_im)
    den = a_re * a_re + a_im * a_im
    nr = abar_re - 1.0
    ni = abar_im
    coef_re = (nr * a_re + ni * a_im) / den
    coef_im = (ni * a_re - nr * a_im) / den
    b_re = b_re_ref[...]
    b_im = b_im_ref[...]
    abar_ref[:, 0:1, :] = abar_re
    abar_ref[:, 1:2, :] = abar_im
    bbar_re_ref[...] = coef_re * b_re - coef_im * b_im
    bbar_im_ref[...] = coef_re * b_im + coef_im * b_re


def _s5_prep(ssm_a_re, ssm_a_im, ssm_log_dt, ssm_b_re, ssm_b_im):
    flat = lambda a: a.reshape(DEPTH, 1, NS)
    chan_major = lambda b: jnp.transpose(b, (0, 3, 1, 2)).reshape(DEPTH, SSM_GROUP, NS)
    vec = jax.ShapeDtypeStruct((DEPTH, 2, NS), F32)
    mat = jax.ShapeDtypeStruct((DEPTH, SSM_GROUP, NS), F32)
    return pl.pallas_call(_s5_prep_kernel, out_shape=(vec, mat, mat), name="s5_prep")(
        flat(ssm_a_re), flat(ssm_a_im), flat(ssm_log_dt), chan_major(ssm_b_re), chan_major(ssm_b_im))


def _mixer_kernel(R, has_init, *refs):
    n_state = 5 if has_init else 0
    x_ref = refs[0]
    states_in = refs[1:1 + n_state]
    (w_in, gxw, gaw, pa, pb, pc, gluw, wout, bblk, cre, cim, vw, vln, vab) = refs[1 + n_state:15 + n_state]
    x1_ref, ca_out, ha_out, cb_out, sre_out, sim_out = refs[15 + n_state:21 + n_state]
    xa_ext, u_ext, us_buf, bu, ha, sre, sim = refs[21 + n_state:]
    T = MIXER_ROW_TILE // R
    TR = MIXER_ROW_TILE
    KA = (LRU_CONV - 1) * R
    KB = (SC_CONV - 1) * R
    i = pl.program_id(0)

    @pl.when(i == 0)
    def _():
        if has_init:
            ca_in, ha_in, cb_in, sre_in, sim_in = states_in
            for k in range(LRU_CONV - 1):
                xa_ext[k * R:(k + 1) * R, :] = ca_in[:, k * WIDTH:(k + 1) * WIDTH]
            for k in range(SC_CONV - 1):
                u_ext[k * R:(k + 1) * R, :] = cb_in[:, k * WIDTH:(k + 1) * WIDTH]
            ha[...] = ha_in[...]
            sre[...] = sre_in[...]
            sim[...] = sim_in[...]
        else:
            xa_ext[0:KA, :] = jnp.zeros((KA, WIDTH), F32)
            u_ext[0:KB, :] = jnp.zeros((KB, WIDTH), F32)
            ha[...] = jnp.zeros((R, WIDTH), F32)
            sre[...] = jnp.zeros((R, NS), F32)
            sim[...] = jnp.zeros((R, NS), F32)

    xb = _load_rows(x_ref).astype(BF16)

    def proj(lo, hi):
        return _dot(xb, w_in[:, lo:hi])

    def vrow(r, sl=slice(None)):
        return vw[r:r + 1, sl]

    parts = [(p * T // TIME_PARTS, (p + 1) * T // TIME_PARTS) for p in range(TIME_PARTS)]

    def taps(ext, w_row, n_taps, t0, t1):
        acc = ext[t0 * R:t1 * R, :] * vrow(w_row)
        for k in range(1, n_taps):
            acc = acc + ext[(t0 + k) * R:(t1 + k) * R, :] * vrow(w_row + k)
        return acc

    us_buf[...] = proj(OFF_US, OFF_GL)
    za = []
    for m in range(WIDTH // B_BLK):
        usb = us_buf[:, m * B_BLK:(m + 1) * B_BLK].astype(BF16)
        bum = _dot(usb, bblk[m])
        bu[:, m * C_BLK:(m + 1) * C_BLK] = bum[:, :C_BLK]
        bu[:, NS + m * C_BLK:NS + (m + 1) * C_BLK] = bum[:, C_BLK:]
        za.append(proj(OFF_XA + m * 2 * B_BLK, OFF_XA + (m + 1) * 2 * B_BLK))
    za = jnp.concatenate(za, axis=1)
    xa_ext[KA:KA + TR, :] = za[:, :WIDTH]
    zb = proj(OFF_SB, OFF_US)
    u_ext[KB:KB + TR, :] = zb[:, WIDTH:2 * WIDTH] * zb[:, 2 * WIDTH:]

    softplus_neg_lam = jax.nn.softplus(-vrow(ROW_LAM))
    gate_in = []
    for t0, t1 in parts:
        xa_c = taps(xa_ext, ROW_CAW, LRU_CONV, t0, t1) + vrow(ROW_CAB)
        pre = []
        for hh in range(WIDTH // GATE_BLK):
            xcb = xa_c[:, hh * GATE_BLK:(hh + 1) * GATE_BLK].astype(BF16)
            pre.append((_dot(xcb, gxw[hh]), _dot(xcb, gaw[hh])))
        gate_in.append((xa_c, pre))

    h_lru = ha[...]
    lhs_a = []
    for (t0, t1), (xa_c, pre) in zip(parts, gate_in):
        a_in, b_in = [], []
        for hh in range(WIDTH // GATE_BLK):
            sl = slice(hh * GATE_BLK, (hh + 1) * GATE_BLK)
            gx = jax.nn.sigmoid(pre[hh][0] + vrow(ROW_GXB, sl))
            ga = jax.nn.sigmoid(pre[hh][1] + vrow(ROW_GAB, sl))
            a = jnp.exp(-LRU_C * ga * softplus_neg_lam[:, sl])
            a_in.append(a)
            b_in.append(jnp.sqrt(1.0 - a * a) * (gx * xa_c[:, sl]))
        a_in, b_in = jnp.concatenate(a_in, axis=1), jnp.concatenate(b_in, axis=1)
        h_rows = []
        for t in range(t1 - t0):
            h_lru = a_in[t * R:(t + 1) * R, :] * h_lru + b_in[t * R:(t + 1) * R, :]
            h_rows.append(h_lru)
        h_a = jnp.concatenate(h_rows, axis=0)
        lhs_a.append((h_a * jax.nn.gelu(za[t0 * R:t1 * R, WIDTH:])).astype(BF16))
    ha[...] = h_lru

    lhs_b = [(zb[t0 * R:t1 * R, :WIDTH] * taps(u_ext, ROW_CBW, SC_CONV, t0, t1)).astype(BF16) for t0, t1 in parts]

    slab = max(V7X_LANES, min(C_BLK, S5_SLAB_ELEMS // R))
    slabs = [(slice(lo, lo + slab), slice(NS + lo, NS + lo + slab)) for lo in range(0, NS, slab)]
    abar = [(jnp.broadcast_to(vab[0:1, sl], (R, slab)), jnp.broadcast_to(vab[1:2, sl], (R, slab))) for sl, _ in slabs]
    h_s5 = [(sre[:, sl], sim[:, sl]) for sl, _ in slabs]
    for t0, t1 in parts:
        for s, (sl, sl_im) in enumerate(slabs):
            (ar, ai), (hr, hi) = abar[s], h_s5[s]
            for t in range(t0, t1):
                step = slice(t * R, (t + 1) * R)
                hr, hi = ar * hr - ai * hi + bu[step, sl], ar * hi + ai * hr + bu[step, sl_im]
                bu[step, sl] = hr
                bu[step, sl_im] = hi
            h_s5[s] = (hr, hi)
    for (sl, _), (hr, hi) in zip(slabs, h_s5):
        sre[:, sl] = hr
        sim[:, sl] = hi

    gates, out_ab, zc, out_c = {}, {}, {}, {}

    def stage_gates(p):
        t0, t1 = parts[p]
        xb_p = xb[t0 * R:t1 * R, :]
        gates[p] = [jax.nn.sigmoid(_dot(xb_p, w_in[:, OFF_GL + j * D_MODEL:OFF_GL + (j + 1) * D_MODEL]))
                    for j in range(3)]

    def stage_branches(p):
        t0, t1 = parts[p]
        rows = slice(t0 * R, t1 * R)
        out_ab[p] = (_dot(lhs_a[p], pa[...]), _dot(lhs_b[p], pb[...]))
        y = []
        for k in range(NS // C_BLK):
            hre = bu[rows, k * C_BLK:(k + 1) * C_BLK].astype(BF16)
            him = bu[rows, NS + k * C_BLK:NS + (k + 1) * C_BLK].astype(BF16)
            sl = slice(k * B_BLK, (k + 1) * B_BLK)
            y.append(_dot(hre, cre[k]) + _dot(him, cim[k]) + vrow(ROW_SSD, sl) * us_buf[rows, sl])
        zc[p] = jax.nn.gelu(jnp.concatenate(y, axis=1))

    def stage_glu(p):
        zc[p] = (zc[p] * jax.nn.sigmoid(_dot(zc[p].astype(BF16), gluw[...]) + vrow(ROW_GLB))).astype(BF16)

    def stage_proj_c(p):
        out_c[p] = _dot(zc[p], pc[...])

    def stage_out(p):
        t0, t1 = parts[p]
        m = gates[p][0] * out_ab[p][0] + gates[p][1] * out_ab[p][1] + gates[p][2] * out_c[p]
        out = _dot(m.astype(BF16), wout[...])
        y1 = _layer_norm(ALPHA * _load_rows(x_ref, t0, t1) + out, vln[0:1, :], vln[1:2, :])
        _store_rows(x1_ref, y1, t0)

    for stage in (stage_gates, stage_branches, stage_glu, stage_proj_c, stage_out):
        for p in range(TIME_PARTS):
            stage(p)

    xa_ext[0:KA, :] = xa_ext[TR:TR + KA, :]
    u_ext[0:KB, :] = u_ext[TR:TR + KB, :]

    @pl.when(i == pl.num_programs(0) - 1)
    def _():
        for k in range(LRU_CONV - 1):
            ca_out[:, k * WIDTH:(k + 1) * WIDTH] = xa_ext[k * R:(k + 1) * R, :]
        for k in range(SC_CONV - 1):
            cb_out[:, k * WIDTH:(k + 1) * WIDTH] = u_ext[k * R:(k + 1) * R, :]
        ha_out[...] = ha[...]
        sre_out[...] = sre[...]
        sim_out[...] = sim[...]


def _layer_block(layer, shape):
    zeros = (0,) * (len(shape) - 1)
    return pl.BlockSpec((None,) + tuple(shape[1:]), lambda i: (layer,) + zeros, pipeline_mode=pl.Buffered(1))


def _mixer(x, layer, states, wts):
    R = x.shape[0]
    tile = MIXER_ROW_TILE
    T = tile // R
    n_steps = x.shape[1] // (T * D_MODEL)
    assert tile % R == 0 and x.shape[1] % (T * D_MODEL) == 0
    x_spec = pl.BlockSpec((R, T * D_MODEL), lambda i: (0, i))
    state_shapes = [(R, (LRU_CONV - 1) * WIDTH), (R, WIDTH), (R, (SC_CONV - 1) * WIDTH), (R, NS), (R, NS)]
    states = () if states is None else tuple(states)
    in_specs = ([x_spec] + [_layer_block(layer, s.shape) for s in states]
                + [_layer_block(layer, w.shape) for w in wts])
    out_shape = [jax.ShapeDtypeStruct(x.shape, F32)] + [jax.ShapeDtypeStruct(s, F32) for s in state_shapes]
    out_specs = [x_spec] + [pl.BlockSpec(s, lambda i: (0, 0)) for s in state_shapes]
    scratch = [
        pltpu.VMEM(((LRU_CONV - 1) * R + tile, WIDTH), F32),
        pltpu.VMEM(((SC_CONV - 1) * R + tile, WIDTH), F32),
        pltpu.VMEM((tile, WIDTH), F32),
        pltpu.VMEM((tile, 2 * NS), F32),
        pltpu.VMEM((R, WIDTH), F32),
        pltpu.VMEM((R, NS), F32),
        pltpu.VMEM((R, NS), F32),
    ]
    outs = pl.pallas_call(
        functools.partial(_mixer_kernel, R, bool(states)),
        grid=(n_steps,), in_specs=in_specs, out_specs=out_specs, out_shape=out_shape,
        scratch_shapes=scratch,
        compiler_params=pltpu.CompilerParams(dimension_semantics=("arbitrary",),
                                             vmem_limit_bytes=V7X_VMEM_LIMIT_BYTES),
        name=f"mixer_r{R}",
    )(x, *states, *wts)
    return outs[0], outs[1:]


def _mlp_block(x_ref, o_ref, up, down, vln):
    r = x_ref.shape[0]
    steps_all = x_ref.shape[1] // D_MODEL
    sub_steps = min(steps_all, MLP_SUB_ROWS // r)
    n_ff = D_FF // FF_BLK
    for s0 in range(0, steps_all, sub_steps):
        x = _load_rows(x_ref, s0, s0 + sub_steps)
        xb = x.astype(BF16)
        acc = ALPHA * x
        for c in range(n_ff):
            h = _dot(xb, up[:, c * FF_BLK:(c + 1) * FF_BLK].astype(BF16))
            h = jnp.square(jnp.maximum(h, 0.0)).astype(BF16)
            if c < n_ff - 1:
                acc = acc + _dot(h, down[c * FF_BLK:(c + 1) * FF_BLK, :].astype(BF16))
        down_last = down[(n_ff - 1) * FF_BLK:, :].astype(BF16)
        for part in range(TAIL_PARTS):
            t0, t1 = part * sub_steps // TAIL_PARTS, (part + 1) * sub_steps // TAIL_PARTS
            rows = slice(t0 * r, t1 * r)
            f = acc[rows, :] + _dot(h[rows, :], down_last)
            _store_rows(o_ref, _layer_norm(f, vln[0:1, :], vln[1:2, :]), s0 + t0)


def _mlp_kernel(n_prompt, xp_ref, xs_ref, up, down, vln, op_ref, os_ref):
    i = pl.program_id(0)

    @pl.when(i < n_prompt)
    def _():
        _mlp_block(xp_ref, op_ref, up, down, vln)

    @pl.when(i == n_prompt)
    def _():
        _mlp_block(xs_ref, os_ref, up, down, vln)


def _mlp(xp, xs, layer, up, down, vln):
    rp, rs = xp.shape[0], xs.shape[0]
    tp = ROW_TILE // rp
    n_prompt = xp.shape[1] // (tp * D_MODEL)
    p_spec = pl.BlockSpec((rp, tp * D_MODEL), lambda i: (0, jnp.minimum(i, n_prompt - 1)))
    s_spec = pl.BlockSpec(xs.shape, lambda i: (0, 0))
    return pl.pallas_call(
        functools.partial(_mlp_kernel, n_prompt),
        grid=(n_prompt + 1,),
        in_specs=[p_spec, s_spec] + [_layer_block(layer, w.shape) for w in (up, down, vln)],
        out_specs=[p_spec, s_spec],
        out_shape=[jax.ShapeDtypeStruct(xp.shape, F32), jax.ShapeDtypeStruct(xs.shape, F32)],
        compiler_params=pltpu.CompilerParams(dimension_semantics=("arbitrary",),
                                             vmem_limit_bytes=V7X_VMEM_LIMIT_BYTES),
        name="mlp",
    )(xp, xs, up, down, vln)


def _block_diag(blocks):
    n, a, b = blocks.shape[-3:]
    eye = jnp.eye(n, dtype=blocks.dtype)
    out = jnp.einsum("...nab,nm->...namb", blocks, eye)
    return out.reshape(*blocks.shape[:-3], n * a, n * b)


def kernel(x_prompt, x_sample, state_lru_conv, state_lru_h, state_sconv, state_ssm_re, state_ssm_im,
           w_in, conv_a_w, conv_a_b, gate_x_w, gate_x_b, gate_a_w, gate_a_b, lru_lambda, conv_b_w,
           ssm_a_re, ssm_a_im, ssm_log_dt, ssm_b_re, ssm_b_im, ssm_c_re, ssm_c_im, ssm_d, glu_w, glu_b,
           proj_a, proj_b, proj_c, w_out, ln1_g, ln1_b, mlp_up, mlp_down, ln2_g, ln2_b):
    batch_p, batch_s = x_prompt.shape[0], x_sample.shape[0]
    bf = lambda w: w.astype(BF16)

    vab, bbar_re, bbar_im = _s5_prep(ssm_a_re, ssm_a_im, ssm_log_dt, ssm_b_re, ssm_b_im)

    def b_blocks(bbar):
        bb = bbar.reshape(DEPTH, SSM_GROUP, WIDTH // B_BLK, B_BLK // SSM_GROUP, SSM_STATE)
        return _block_diag(jnp.transpose(bb, (0, 2, 3, 1, 4)))

    bblk = bf(jnp.concatenate([b_blocks(bbar_re), b_blocks(bbar_im)], axis=-1))

    def c_blocks(c):
        cc = c.reshape(DEPTH, NS // C_BLK, C_BLK // SSM_STATE, SSM_GROUP, SSM_STATE)
        return bf(_block_diag(jnp.swapaxes(cc, -1, -2)))

    cre, cim = c_blocks(ssm_c_re), c_blocks(-ssm_c_im)

    def gate_blocks(w):
        heads = GATE_BLK // LRU_HEAD_DIM
        return bf(_block_diag(w.reshape(DEPTH, LRU_HEADS // heads, heads, LRU_HEAD_DIM, LRU_HEAD_DIM)))

    gxw, gaw = gate_blocks(gate_x_w), gate_blocks(gate_a_w)

    row = lambda v: v[:, None, :]
    vw = jnp.concatenate([conv_a_w, row(conv_a_b), row(gate_x_b), row(gate_a_b), row(lru_lambda),
                          conv_b_w, row(ssm_d), row(glu_b), jnp.zeros((DEPTH, 3, WIDTH), F32)], axis=1)
    vln1 = jnp.stack([ln1_g, ln1_b], axis=1)
    vln2 = jnp.stack([ln2_g, ln2_b], axis=1)
    wts = (bf(w_in), gxw, gaw, bf(proj_a), bf(proj_b), bf(proj_c), bf(glu_w), bf(w_out), bblk, cre, cim,
           vw, vln1, vab)
    up, down = mlp_up, mlp_down

    states_s = (state_lru_conv.reshape(DEPTH, batch_s, (LRU_CONV - 1) * WIDTH), state_lru_h,
                state_sconv.reshape(DEPTH, batch_s, (SC_CONV - 1) * WIDTH),
                state_ssm_re.reshape(DEPTH, batch_s, NS), state_ssm_im.reshape(DEPTH, batch_s, NS))

    yp = x_prompt.reshape(batch_p, -1)
    ys = x_sample.reshape(batch_s, -1)
    new_p, new_s = [], []
    for layer in range(DEPTH):
        ys, sts = _mixer(ys, layer, states_s, wts)
        yp, stp = _mixer(yp, layer, None, wts)
        yp, ys = _mlp(yp, ys, layer, up, down, vln2)
        new_p.append(stp)
        new_s.append(sts)

    def collect(new, batch):
        ca, ha, cb, sre, sim = (jnp.stack([st[j] for st in new], 0) for j in range(5))
        shape_ssm = (DEPTH, batch, SSM_GROUPS, SSM_STATE)
        return (ca.reshape(DEPTH, batch, LRU_CONV - 1, WIDTH), ha, cb.reshape(DEPTH, batch, SC_CONV - 1, WIDTH),
                sre.reshape(shape_ssm), sim.reshape(shape_ssm))

    return ((yp.reshape(x_prompt.shape), ys.reshape(x_sample.shape))
            + collect(new_p, batch_p) + collect(new_s, batch_s))
```

```python
import functools

import jax
import jax.numpy as jnp
from jax import lax
from jax.experimental import pallas as pl
from jax.experimental.pallas import tpu as pltpu

F32 = jnp.float32
BF16 = jnp.bfloat16

D_MODEL = 1024
DEPTH = 2
WIDTH = 512
LRU_HEADS = 8
LRU_HEAD_DIM = WIDTH // LRU_HEADS
LRU_CONV = 4
LRU_C = 8.0
SC_CONV = 3
SSM_GROUP = 16
SSM_GROUPS = WIDTH // SSM_GROUP
SSM_STATE = 64
NS = SSM_GROUPS * SSM_STATE
D_FF = 4 * D_MODEL
N_IN = 6 * WIDTH + 3 * D_MODEL
ALPHA = (2 * DEPTH) ** 0.25
LN_EPS = 1e-5

OFF_XA, OFF_YA, OFF_SB, OFF_SC, OFF_SH, OFF_US, OFF_GL = 0, 512, 1024, 1536, 2048, 2560, 3072

V7X_LANES = 128
V7X_MXU_DIM = 256
V7X_VMEM_LIMIT_BYTES = 58 * 1024 * 1024

ROW_TILE = 1024
MLP_SUB_ROWS = 512
MIXER_ROW_TILE = 512
GATE_BLK = V7X_MXU_DIM
B_BLK = V7X_LANES
C_BLK = B_BLK // SSM_GROUP * SSM_STATE
S5_SLAB_ELEMS = 4096
FF_BLK = 1024
TAIL_PARTS = 4
TIME_PARTS = 2

ROW_CAW, ROW_CAB, ROW_GXB, ROW_GAB, ROW_LAM, ROW_CBW, ROW_SSD, ROW_GLB = 0, 4, 5, 6, 7, 8, 11, 12


def _dot(a, b):
    return jnp.dot(a, b, preferred_element_type=F32)


def _sigmoid(x):
    return 0.5 * jnp.tanh(0.5 * x) + 0.5


def _layer_norm(x, g, b):
    mu = jnp.mean(x, axis=-1, keepdims=True)
    xc = x - mu
    var = jnp.mean(xc * xc, axis=-1, keepdims=True)
    return xc * lax.rsqrt(var + LN_EPS) * g + b


def _load_rows(x_ref, t0=0, t1=None):
    t1 = x_ref.shape[1] // D_MODEL if t1 is None else t1
    return jnp.concatenate([x_ref[:, t * D_MODEL:(t + 1) * D_MODEL] for t in range(t0, t1)], axis=0)


def _store_rows(o_ref, rows, t0=0):
    r = o_ref.shape[0]
    for t in range(rows.shape[0] // r):
        o_ref[:, (t0 + t) * D_MODEL:(t0 + t + 1) * D_MODEL] = rows[t * r:(t + 1) * r, :]


def _s5_prep_kernel(a_re_ref, a_im_ref, log_dt_ref, b_re_ref, b_im_ref,
                    abar_ref, bbar_re_ref, bbar_im_ref):
    a_re = a_re_ref[...]
    a_im = a_im_ref[...]
    step = jnp.exp(log_dt_ref[...])
    mag = jnp.exp(step * a_re)
    abar_re = mag * jnp.cos(step * a_im)
    abar_im = mag * jnp.sin(step * a_im)
    den = a_re * a_re + a_im * a_im
    nr = abar_re - 1.0
    ni = abar_im
    coef_re = (nr * a_re + ni * a_im) / den
    coef_im = (ni * a_re - nr * a_im) / den
    b_re = b_re_ref[...]
    b_im = b_im_ref[...]
    abar_ref[:, 0:1, :] = abar_re
    abar_ref[:, 1:2, :] = abar_im
    bbar_re_ref[...] = coef_re * b_re - coef_im * b_im
    bbar_im_ref[...] = coef_re * b_im + coef_im * b_re


def _s5_prep(ssm_a_re, ssm_a_im, ssm_log_dt, ssm_b_re, ssm_b_im):
    flat = lambda a: a.reshape(DEPTH, 1, NS)
    chan_major = lambda b: jnp.transpose(b, (0, 3, 1, 2)).reshape(DEPTH, SSM_GROUP, NS)
    vec = jax.ShapeDtypeStruct((DEPTH, 2, NS), F32)
    mat = jax.ShapeDtypeStruct((DEPTH, SSM_GROUP, NS), F32)
    return pl.pallas_call(_s5_prep_kernel, out_shape=(vec, mat, mat), name="s5_prep")(
        flat(ssm_a_re), flat(ssm_a_im), flat(ssm_log_dt), chan_major(ssm_b_re), chan_major(ssm_b_im))


def _mixer_kernel(R, has_init, *refs):
    n_state = 5 if has_init else 0
    x_ref = refs[0]
    states_in = refs[1:1 + n_state]
    (w_in, gxw, gaw, pa, pb, pc, gluw, wout, bblk, cre, cim, vw, vln, vab) = refs[1 + n_state:15 + n_state]
    x1_ref, ca_out, ha_out, cb_out, sre_out, sim_out = refs[15 + n_state:21 + n_state]
    xa_ext, u_ext, us_buf, bu, ha, sre, sim = refs[21 + n_state:]
    T = MIXER_ROW_TILE // R
    TR = MIXER_ROW_TILE
    KA = (LRU_CONV - 1) * R
    KB = (SC_CONV - 1) * R
    i = pl.program_id(0)

    @pl.when(i == 0)
    def _():
        if has_init:
            ca_in, ha_in, cb_in, sre_in, sim_in = states_in
            for k in range(LRU_CONV - 1):
                xa_ext[k * R:(k + 1) * R, :] = ca_in[:, k * WIDTH:(k + 1) * WIDTH]
            for k in range(SC_CONV - 1):
                u_ext[k * R:(k + 1) * R, :] = cb_in[:, k * WIDTH:(k + 1) * WIDTH]
            ha[...] = ha_in[...]
            sre[...] = sre_in[...]
            sim[...] = sim_in[...]
        else:
            xa_ext[0:KA, :] = jnp.zeros((KA, WIDTH), F32)
            u_ext[0:KB, :] = jnp.zeros((KB, WIDTH), F32)
            ha[...] = jnp.zeros((R, WIDTH), F32)
            sre[...] = jnp.zeros((R, NS), F32)
            sim[...] = jnp.zeros((R, NS), F32)

    xb = _load_rows(x_ref).astype(BF16)

    def proj(lo, hi):
        return _dot(xb, w_in[:, lo:hi])

    def vrow(r, sl=slice(None)):
        return vw[r:r + 1, sl]

    parts = [(p * T // TIME_PARTS, (p + 1) * T // TIME_PARTS) for p in range(TIME_PARTS)]

    def taps(ext, w_row, n_taps, t0, t1):
        acc = ext[t0 * R:t1 * R, :] * vrow(w_row)
        for k in range(1, n_taps):
            acc = acc + ext[(t0 + k) * R:(t1 + k) * R, :] * vrow(w_row + k)
        return acc

    us_buf[...] = proj(OFF_US, OFF_GL)
    za = []
    for m in range(WIDTH // B_BLK):
        usb = us_buf[:, m * B_BLK:(m + 1) * B_BLK].astype(BF16)
        bum = _dot(usb, bblk[m])
        bu[:, m * C_BLK:(m + 1) * C_BLK] = bum[:, :C_BLK]
        bu[:, NS + m * C_BLK:NS + (m + 1) * C_BLK] = bum[:, C_BLK:]
        za.append(proj(OFF_XA + m * 2 * B_BLK, OFF_XA + (m + 1) * 2 * B_BLK))
    za = jnp.concatenate(za, axis=1)
    xa_ext[KA:KA + TR, :] = za[:, :WIDTH]
    zb = proj(OFF_SB, OFF_US)
    u_ext[KB:KB + TR, :] = zb[:, WIDTH:2 * WIDTH] * zb[:, 2 * WIDTH:]

    softplus_neg_lam = jax.nn.softplus(-vrow(ROW_LAM))
    gate_in = []
    for t0, t1 in parts:
        xa_c = taps(xa_ext, ROW_CAW, LRU_CONV, t0, t1) + vrow(ROW_CAB)
        pre = []
        for hh in range(WIDTH // GATE_BLK):
            xcb = xa_c[:, hh * GATE_BLK:(hh + 1) * GATE_BLK].astype(BF16)
            pre.append((_dot(xcb, gxw[hh]), _dot(xcb, gaw[hh])))
        gate_in.append((xa_c, pre))

    h_lru = ha[...]
    lhs_a = []
    for (t0, t1), (xa_c, pre) in zip(parts, gate_in):
        a_in, b_in = [], []
        for hh in range(WIDTH // GATE_BLK):
            sl = slice(hh * GATE_BLK, (hh + 1) * GATE_BLK)
            gx = _sigmoid(pre[hh][0] + vrow(ROW_GXB, sl))
            ga = _sigmoid(pre[hh][1] + vrow(ROW_GAB, sl))
            a = jnp.exp(-LRU_C * ga * softplus_neg_lam[:, sl])
            a_in.append(a)
            b_in.append(jnp.sqrt(1.0 - a * a) * (gx * xa_c[:, sl]))
        a_in, b_in = jnp.concatenate(a_in, axis=1), jnp.concatenate(b_in, axis=1)
        h_rows = []
        for t in range(t1 - t0):
            h_lru = a_in[t * R:(t + 1) * R, :] * h_lru + b_in[t * R:(t + 1) * R, :]
            h_rows.append(h_lru)
        h_a = jnp.concatenate(h_rows, axis=0)
        lhs_a.append((h_a * jax.nn.gelu(za[t0 * R:t1 * R, WIDTH:])).astype(BF16))
    ha[...] = h_lru

    lhs_b = [(zb[t0 * R:t1 * R, :WIDTH] * taps(u_ext, ROW_CBW, SC_CONV, t0, t1)).astype(BF16) for t0, t1 in parts]

    slab = max(V7X_LANES, min(C_BLK, S5_SLAB_ELEMS // R))
    slabs = [(slice(lo, lo + slab), slice(NS + lo, NS + lo + slab)) for lo in range(0, NS, slab)]
    abar = [(jnp.broadcast_to(vab[0:1, sl], (R, slab)), jnp.broadcast_to(vab[1:2, sl], (R, slab))) for sl, _ in slabs]
    h_s5 = [(sre[:, sl], sim[:, sl]) for sl, _ in slabs]
    for t0, t1 in parts:
        for s, (sl, sl_im) in enumerate(slabs):
            (ar, ai), (hr, hi) = abar[s], h_s5[s]
            for t in range(t0, t1):
                step = slice(t * R, (t + 1) * R)
                hr, hi = ar * hr - ai * hi + bu[step, sl], ar * hi + ai * hr + bu[step, sl_im]
                bu[step, sl] = hr
                bu[step, sl_im] = hi
            h_s5[s] = (hr, hi)
    for (sl, _), (hr, hi) in zip(slabs, h_s5):
        sre[:, sl] = hr
        sim[:, sl] = hi

    gates, out_ab, zc, out_c = {}, {}, {}, {}

    gates_full = [_sigmoid(proj(OFF_GL + j * D_MODEL, OFF_GL + (j + 1) * D_MODEL)) for j in range(3)]

    def stage_gates(p):
        t0, t1 = parts[p]
        gates[p] = [g[t0 * R:t1 * R, :] for g in gates_full]

    def stage_branches(p):
        t0, t1 = parts[p]
        rows = slice(t0 * R, t1 * R)
        out_ab[p] = (_dot(lhs_a[p], pa[...]), _dot(lhs_b[p], pb[...]))
        y = []
        for k in range(NS // C_BLK):
            hre = bu[rows, k * C_BLK:(k + 1) * C_BLK].astype(BF16)
            him = bu[rows, NS + k * C_BLK:NS + (k + 1) * C_BLK].astype(BF16)
            sl = slice(k * B_BLK, (k + 1) * B_BLK)
            y.append(_dot(hre, cre[k]) + _dot(him, cim[k]) + vrow(ROW_SSD, sl) * us_buf[rows, sl])
        zc[p] = jax.nn.gelu(jnp.concatenate(y, axis=1))

    def stage_glu(p):
        zc[p] = (zc[p] * _sigmoid(_dot(zc[p].astype(BF16), gluw[...]) + vrow(ROW_GLB))).astype(BF16)

    def stage_proj_c(p):
        out_c[p] = _dot(zc[p], pc[...])

    def stage_out(p):
        t0, t1 = parts[p]
        m = gates[p][0] * out_ab[p][0] + gates[p][1] * out_ab[p][1] + gates[p][2] * out_c[p]
        out = _dot(m.astype(BF16), wout[...])
        y1 = _layer_norm(ALPHA * _load_rows(x_ref, t0, t1) + out, vln[0:1, :], vln[1:2, :])
        _store_rows(x1_ref, y1, t0)

    for stage in (stage_gates, stage_branches, stage_glu, stage_proj_c, stage_out):
        for p in range(TIME_PARTS):
            stage(p)

    xa_ext[0:KA, :] = xa_ext[TR:TR + KA, :]
    u_ext[0:KB, :] = u_ext[TR:TR + KB, :]

    @pl.when(i == pl.num_programs(0) - 1)
    def _():
        for k in range(LRU_CONV - 1):
            ca_out[:, k * WIDTH:(k + 1) * WIDTH] = xa_ext[k * R:(k + 1) * R, :]
        for k in range(SC_CONV - 1):
            cb_out[:, k * WIDTH:(k + 1) * WIDTH] = u_ext[k * R:(k + 1) * R, :]
        ha_out[...] = ha[...]
        sre_out[...] = sre[...]
        sim_out[...] = sim[...]


def _layer_block(layer, shape):
    zeros = (0,) * (len(shape) - 1)
    return pl.BlockSpec((None,) + tuple(shape[1:]), lambda i: (layer,) + zeros, pipeline_mode=pl.Buffered(1))


def _mixer(x, layer, states, wts):
    R = x.shape[0]
    tile = MIXER_ROW_TILE
    T = tile // R
    n_steps = x.shape[1] // (T * D_MODEL)
    assert tile % R == 0 and x.shape[1] % (T * D_MODEL) == 0
    x_spec = pl.BlockSpec((R, T * D_MODEL), lambda i: (0, i))
    state_shapes = [(R, (LRU_CONV - 1) * WIDTH), (R, WIDTH), (R, (SC_CONV - 1) * WIDTH), (R, NS), (R, NS)]
    states = () if states is None else tuple(states)
    x_in_spec = pl.BlockSpec(x_spec.block_shape, x_spec.index_map, pipeline_mode=pl.Buffered(1)) if n_steps == 1 else x_spec
    in_specs = ([x_in_spec] + [_layer_block(layer, s.shape) for s in states]
                + [_layer_block(layer, w.shape) for w in wts])
    out_shape = [jax.ShapeDtypeStruct(x.shape, F32)] + [jax.ShapeDtypeStruct(s, F32) for s in state_shapes]
    out_specs = [x_spec] + [pl.BlockSpec(s, lambda i: (0, 0)) for s in state_shapes]
    scratch = [
        pltpu.VMEM(((LRU_CONV - 1) * R + tile, WIDTH), F32),
        pltpu.VMEM(((SC_CONV - 1) * R + tile, WIDTH), F32),
        pltpu.VMEM((tile, WIDTH), F32),
        pltpu.VMEM((tile, 2 * NS), F32),
        pltpu.VMEM((R, WIDTH), F32),
        pltpu.VMEM((R, NS), F32),
        pltpu.VMEM((R, NS), F32),
    ]
    outs = pl.pallas_call(
        functools.partial(_mixer_kernel, R, bool(states)),
        grid=(n_steps,), in_specs=in_specs, out_specs=out_specs, out_shape=out_shape,
        scratch_shapes=scratch,
        compiler_params=pltpu.CompilerParams(dimension_semantics=("arbitrary",),
                                             vmem_limit_bytes=V7X_VMEM_LIMIT_BYTES),
        name=f"mixer_r{R}",
    )(x, *states, *wts)
    return outs[0], outs[1:]


def _mlp_block(x_ref, o_ref, up, down, vln):
    r = x_ref.shape[0]
    steps_all = x_ref.shape[1] // D_MODEL
    sub_steps = min(steps_all, MLP_SUB_ROWS // r)
    n_ff = D_FF // FF_BLK
    for s0 in range(0, steps_all, sub_steps):
        x = _load_rows(x_ref, s0, s0 + sub_steps)
        xb = x.astype(BF16)
        acc = ALPHA * x
        for c in range(n_ff):
            h = _dot(xb, up[:, c * FF_BLK:(c + 1) * FF_BLK].astype(BF16))
            h = jnp.square(jnp.maximum(h, 0.0)).astype(BF16)
            if c < n_ff - 1:
                acc = acc + _dot(h, down[c * FF_BLK:(c + 1) * FF_BLK, :].astype(BF16))
        down_last = down[(n_ff - 1) * FF_BLK:, :].astype(BF16)
        for part in range(TAIL_PARTS):
            t0, t1 = part * sub_steps // TAIL_PARTS, (part + 1) * sub_steps // TAIL_PARTS
            rows = slice(t0 * r, t1 * r)
            f = acc[rows, :] + _dot(h[rows, :], down_last)
            _store_rows(o_ref, _layer_norm(f, vln[0:1, :], vln[1:2, :]), s0 + t0)


def _mlp_kernel(n_prompt, xp_ref, xs_ref, up, down, vln, op_ref, os_ref):
    i = pl.program_id(0)

    @pl.when(i < n_prompt)
    def _():
        _mlp_block(xp_ref, op_ref, up, down, vln)

    @pl.when(i == n_prompt)
    def _():
        _mlp_block(xs_ref, os_ref, up, down, vln)


def _mlp(xp, xs, layer, up, down, vln):
    rp, rs = xp.shape[0], xs.shape[0]
    tp = ROW_TILE // rp
    n_prompt = xp.shape[1] // (tp * D_MODEL)
    p_spec = pl.BlockSpec((rp, tp * D_MODEL), lambda i: (0, jnp.minimum(i, n_prompt - 1)))
    s_spec = pl.BlockSpec(xs.shape, lambda i: (0, 0))
    return pl.pallas_call(
        functools.partial(_mlp_kernel, n_prompt),
        grid=(n_prompt + 1,),
        in_specs=[p_spec, pl.BlockSpec(xs.shape, lambda i: (0, 0), pipeline_mode=pl.Buffered(1))]
        + [_layer_block(layer, w.shape) for w in (up, down, vln)],
        out_specs=[p_spec, s_spec],
        out_shape=[jax.ShapeDtypeStruct(xp.shape, F32), jax.ShapeDtypeStruct(xs.shape, F32)],
        compiler_params=pltpu.CompilerParams(dimension_semantics=("arbitrary",),
                                             vmem_limit_bytes=V7X_VMEM_LIMIT_BYTES),
        name="mlp",
    )(xp, xs, up, down, vln)


def _block_diag(blocks):
    n, a, b = blocks.shape[-3:]
    eye = jnp.eye(n, dtype=blocks.dtype)
    out = jnp.einsum("...nab,nm->...namb", blocks, eye)
    return out.reshape(*blocks.shape[:-3], n * a, n * b)


def kernel(x_prompt, x_sample, state_lru_conv, state_lru_h, state_sconv, state_ssm_re, state_ssm_im,
           w_in, conv_a_w, conv_a_b, gate_x_w, gate_x_b, gate_a_w, gate_a_b, lru_lambda, conv_b_w,
           ssm_a_re, ssm_a_im, ssm_log_dt, ssm_b_re, ssm_b_im, ssm_c_re, ssm_c_im, ssm_d, glu_w, glu_b,
           proj_a, proj_b, proj_c, w_out, ln1_g, ln1_b, mlp_up, mlp_down, ln2_g, ln2_b):
    batch_p, batch_s = x_prompt.shape[0], x_sample.shape[0]
    bf = lambda w: w.astype(BF16)

    vab, bbar_re, bbar_im = _s5_prep(ssm_a_re, ssm_a_im, ssm_log_dt, ssm_b_re, ssm_b_im)

    def b_blocks(bbar):
        bb = bbar.reshape(DEPTH, SSM_GROUP, WIDTH // B_BLK, B_BLK // SSM_GROUP, SSM_STATE)
        return _block_diag(jnp.transpose(bb, (0, 2, 3, 1, 4)))

    bblk = bf(jnp.concatenate([b_blocks(bbar_re), b_blocks(bbar_im)], axis=-1))

    def c_blocks(c):
        cc = c.reshape(DEPTH, NS // C_BLK, C_BLK // SSM_STATE, SSM_GROUP, SSM_STATE)
        return bf(_block_diag(jnp.swapaxes(cc, -1, -2)))

    cre, cim = c_blocks(ssm_c_re), c_blocks(-ssm_c_im)

    def gate_blocks(w):
        heads = GATE_BLK // LRU_HEAD_DIM
        return bf(_block_diag(w.reshape(DEPTH, LRU_HEADS // heads, heads, LRU_HEAD_DIM, LRU_HEAD_DIM)))

    gxw, gaw = gate_blocks(gate_x_w), gate_blocks(gate_a_w)

    row = lambda v: v[:, None, :]
    vw = jnp.concatenate([conv_a_w, row(conv_a_b), row(gate_x_b), row(gate_a_b), row(lru_lambda),
                          conv_b_w, row(ssm_d), row(glu_b), jnp.zeros((DEPTH, 3, WIDTH), F32)], axis=1)
    vln1 = jnp.stack([ln1_g, ln1_b], axis=1)
    vln2 = jnp.stack([ln2_g, ln2_b], axis=1)
    wts = (bf(w_in), gxw, gaw, bf(proj_a), bf(proj_b), bf(proj_c), bf(glu_w), bf(w_out), bblk, cre, cim,
           vw, vln1, vab)
    up, down = mlp_up, mlp_down

    states_s = (state_lru_conv.reshape(DEPTH, batch_s, (LRU_CONV - 1) * WIDTH), state_lru_h,
                state_sconv.reshape(DEPTH, batch_s, (SC_CONV - 1) * WIDTH),
                state_ssm_re.reshape(DEPTH, batch_s, NS), state_ssm_im.reshape(DEPTH, batch_s, NS))

    yp = x_prompt.reshape(batch_p, -1)
    ys = x_sample.reshape(batch_s, -1)
    new_p, new_s = [], []
    for layer in range(DEPTH):
        ys, sts = _mixer(ys, layer, states_s, wts)
        yp, stp = _mixer(yp, layer, None, wts)
        yp, ys = _mlp(yp, ys, layer, up, down, vln2)
        new_p.append(stp)
        new_s.append(sts)

    def collect(new, batch):
        ca, ha, cb, sre, sim = (jnp.stack([st[j] for st in new], 0) for j in range(5))
        shape_ssm = (DEPTH, batch, SSM_GROUPS, SSM_STATE)
        return (ca.reshape(DEPTH, batch, LRU_CONV - 1, WIDTH), ha, cb.reshape(DEPTH, batch, SC_CONV - 1, WIDTH),
                sre.reshape(shape_ssm), sim.reshape(shape_ssm))

    return ((yp.reshape(x_prompt.shape), ys.reshape(x_sample.shape))
            + collect(new_p, batch_p) + collect(new_s, batch_s))
```

```python
import functools

import jax
import jax.numpy as jnp
from jax import lax
from jax.experimental import pallas as pl
from jax.experimental.pallas import tpu as pltpu

F32 = jnp.float32
BF16 = jnp.bfloat16

D_MODEL = 1024
DEPTH = 2
WIDTH = 512
LRU_HEADS = 8
LRU_HEAD_DIM = WIDTH // LRU_HEADS
LRU_CONV = 4
LRU_C = 8.0
SC_CONV = 3
SSM_GROUP = 16
SSM_GROUPS = WIDTH // SSM_GROUP
SSM_STATE = 64
NS = SSM_GROUPS * SSM_STATE
D_FF = 4 * D_MODEL
N_IN = 6 * WIDTH + 3 * D_MODEL
ALPHA = (2 * DEPTH) ** 0.25
LN_EPS = 1e-5

OFF_XA, OFF_YA, OFF_SB, OFF_SC, OFF_SH, OFF_US, OFF_GL = 0, 512, 1024, 1536, 2048, 2560, 3072

V7X_LANES = 128
V7X_MXU_DIM = 256
V7X_VMEM_LIMIT_BYTES = 58 * 1024 * 1024

ROW_TILE = 1024
MLP_SUB_ROWS = 512
MIXER_ROW_TILE = 512
GATE_BLK = V7X_MXU_DIM
B_BLK = V7X_LANES
C_BLK = B_BLK // SSM_GROUP * SSM_STATE
S5_SLAB_ELEMS = 4096
FF_BLK = 1024
TAIL_PARTS = 4
TIME_PARTS = 2

ROW_CAW, ROW_CAB, ROW_GXB, ROW_GAB, ROW_LAM, ROW_CBW, ROW_SSD, ROW_GLB = 0, 4, 5, 6, 7, 8, 11, 12


def _dot(a, b):
    return jnp.dot(a, b, preferred_element_type=F32)


def _sigmoid(x):
    return 0.5 * jnp.tanh(0.5 * x) + 0.5


def _layer_norm(x, g, b):
    mu = jnp.mean(x, axis=-1, keepdims=True)
    xc = x - mu
    var = jnp.mean(xc * xc, axis=-1, keepdims=True)
    return xc * lax.rsqrt(var + LN_EPS) * g + b


def _load_rows(x_ref, t0=0, t1=None):
    t1 = x_ref.shape[1] // D_MODEL if t1 is None else t1
    return jnp.concatenate([x_ref[:, t * D_MODEL:(t + 1) * D_MODEL] for t in range(t0, t1)], axis=0)


def _store_rows(o_ref, rows, t0=0):
    r = o_ref.shape[0]
    for t in range(rows.shape[0] // r):
        o_ref[:, (t0 + t) * D_MODEL:(t0 + t + 1) * D_MODEL] = rows[t * r:(t + 1) * r, :]


def _s5_prep_kernel(a_re_ref, a_im_ref, log_dt_ref, b_re_ref, b_im_ref,
                    abar_ref, bbar_re_ref, bbar_im_ref):
    a_re = a_re_ref[...]
    a_im = a_im_ref[...]
    step = jnp.exp(log_dt_ref[...])
    mag = jnp.exp(step * a_re)
    abar_re = mag * jnp.cos(step * a_im)
    abar_im = mag * jnp.sin(step * a_im)
    den = a_re * a_re + a_im * a_im
    nr = abar_re - 1.0
    ni = abar_im
    coef_re = (nr * a_re + ni * a_im) / den
    coef_im = (ni * a_re - nr * a_im) / den
    b_re = b_re_ref[...]
    b_im = b_im_ref[...]
    abar_ref[:, 0:1, :] = abar_re
    abar_ref[:, 1:2, :] = abar_im
    bbar_re_ref[...] = coef_re * b_re - coef_im * b_im
    bbar_im_ref[...] = coef_re * b_im + coef_im * b_re


def _s5_prep(ssm_a_re, ssm_a_im, ssm_log_dt, ssm_b_re, ssm_b_im):
    flat = lambda a: a.reshape(DEPTH, 1, NS)
    chan_major = lambda b: jnp.transpose(b, (0, 3, 1, 2)).reshape(DEPTH, SSM_GROUP, NS)
    vec = jax.ShapeDtypeStruct((DEPTH, 2, NS), F32)
    mat = jax.ShapeDtypeStruct((DEPTH, SSM_GROUP, NS), F32)
    return pl.pallas_call(_s5_prep_kernel, out_shape=(vec, mat, mat), name="s5_prep")(
        flat(ssm_a_re), flat(ssm_a_im), flat(ssm_log_dt), chan_major(ssm_b_re), chan_major(ssm_b_im))


def _mixer_kernel(R, has_init, *refs):
    n_state = 5 if has_init else 0
    x_ref = refs[0]
    states_in = refs[1:1 + n_state]
    (w_in, gxw, gaw, pa, pb, pc, gluw, wout, bblk, cre, cim, vw, vln, vab) = refs[1 + n_state:15 + n_state]
    x1_ref, ca_out, ha_out, cb_out, sre_out, sim_out = refs[15 + n_state:21 + n_state]
    xa_ext, u_ext, us_buf, bu, ha, sre, sim = refs[21 + n_state:]
    T = MIXER_ROW_TILE // R
    TR = MIXER_ROW_TILE
    KA = (LRU_CONV - 1) * R
    KB = (SC_CONV - 1) * R
    i = pl.program_id(0)

    @pl.when(i == 0)
    def _():
        if has_init:
            ca_in, ha_in, cb_in, sre_in, sim_in = states_in
            for k in range(LRU_CONV - 1):
                xa_ext[k * R:(k + 1) * R, :] = ca_in[:, k * WIDTH:(k + 1) * WIDTH]
            for k in range(SC_CONV - 1):
                u_ext[k * R:(k + 1) * R, :] = cb_in[:, k * WIDTH:(k + 1) * WIDTH]
            ha[...] = ha_in[...]
            sre[...] = sre_in[...]
            sim[...] = sim_in[...]
        else:
            xa_ext[0:KA, :] = jnp.zeros((KA, WIDTH), F32)
            u_ext[0:KB, :] = jnp.zeros((KB, WIDTH), F32)
            ha[...] = jnp.zeros((R, WIDTH), F32)
            sre[...] = jnp.zeros((R, NS), F32)
            sim[...] = jnp.zeros((R, NS), F32)

    xb = _load_rows(x_ref).astype(BF16)

    def proj(lo, hi):
        return _dot(xb, w_in[:, lo:hi])

    def vrow(r, sl=slice(None)):
        return vw[r:r + 1, sl]

    parts = [(p * T // TIME_PARTS, (p + 1) * T // TIME_PARTS) for p in range(TIME_PARTS)]

    def taps(ext, w_row, n_taps, t0, t1):
        acc = ext[t0 * R:t1 * R, :] * vrow(w_row)
        for k in range(1, n_taps):
            acc = acc + ext[(t0 + k) * R:(t1 + k) * R, :] * vrow(w_row + k)
        return acc

    us_buf[...] = proj(OFF_US, OFF_GL)
    za = []
    for m in range(WIDTH // B_BLK):
        usb = us_buf[:, m * B_BLK:(m + 1) * B_BLK].astype(BF16)
        bum = _dot(usb, bblk[m])
        bu[:, m * C_BLK:(m + 1) * C_BLK] = bum[:, :C_BLK]
        bu[:, NS + m * C_BLK:NS + (m + 1) * C_BLK] = bum[:, C_BLK:]
        za.append(proj(OFF_XA + m * 2 * B_BLK, OFF_XA + (m + 1) * 2 * B_BLK))
    za = jnp.concatenate(za, axis=1)
    xa_ext[KA:KA + TR, :] = za[:, :WIDTH]
    zb = proj(OFF_SB, OFF_US)
    u_ext[KB:KB + TR, :] = zb[:, WIDTH:2 * WIDTH] * zb[:, 2 * WIDTH:]

    softplus_neg_lam = jax.nn.softplus(-vrow(ROW_LAM))
    gate_in = []
    for t0, t1 in parts:
        xa_c = taps(xa_ext, ROW_CAW, LRU_CONV, t0, t1) + vrow(ROW_CAB)
        pre = []
        for hh in range(WIDTH // GATE_BLK):
            xcb = xa_c[:, hh * GATE_BLK:(hh + 1) * GATE_BLK].astype(BF16)
            pre.append((_dot(xcb, gxw[hh]), _dot(xcb, gaw[hh])))
        gate_in.append((xa_c, pre))

    h_lru = ha[...]
    lhs_a = []
    for (t0, t1), (xa_c, pre) in zip(parts, gate_in):
        a_in, b_in = [], []
        for hh in range(WIDTH // GATE_BLK):
            sl = slice(hh * GATE_BLK, (hh + 1) * GATE_BLK)
            gx = _sigmoid(pre[hh][0] + vrow(ROW_GXB, sl))
            ga = _sigmoid(pre[hh][1] + vrow(ROW_GAB, sl))
            a = jnp.exp(-LRU_C * ga * softplus_neg_lam[:, sl])
            a_in.append(a)
            b_in.append(jnp.sqrt(1.0 - a * a) * (gx * xa_c[:, sl]))
        a_in, b_in = jnp.concatenate(a_in, axis=1), jnp.concatenate(b_in, axis=1)
        h_rows = []
        for t in range(t1 - t0):
            h_lru = a_in[t * R:(t + 1) * R, :] * h_lru + b_in[t * R:(t + 1) * R, :]
            h_rows.append(h_lru)
        h_a = jnp.concatenate(h_rows, axis=0)
        lhs_a.append((h_a * jax.nn.gelu(za[t0 * R:t1 * R, WIDTH:])).astype(BF16))
    ha[...] = h_lru

    lhs_b = [(zb[t0 * R:t1 * R, :WIDTH] * taps(u_ext, ROW_CBW, SC_CONV, t0, t1)).astype(BF16) for t0, t1 in parts]

    slab = max(V7X_LANES, min(C_BLK, S5_SLAB_ELEMS // R))
    slabs = [(slice(lo, lo + slab), slice(NS + lo, NS + lo + slab)) for lo in range(0, NS, slab)]
    abar = [(jnp.broadcast_to(vab[0:1, sl], (R, slab)), jnp.broadcast_to(vab[1:2, sl], (R, slab))) for sl, _ in slabs]
    h_s5 = [(sre[:, sl], sim[:, sl]) for sl, _ in slabs]
    for t0, t1 in parts:
        for s, (sl, sl_im) in enumerate(slabs):
            (ar, ai), (hr, hi) = abar[s], h_s5[s]
            for t in range(t0, t1):
                step = slice(t * R, (t + 1) * R)
                hr, hi = ar * hr - ai * hi + bu[step, sl], ar * hi + ai * hr + bu[step, sl_im]
                bu[step, sl] = hr
                bu[step, sl_im] = hi
            h_s5[s] = (hr, hi)
    for (sl, _), (hr, hi) in zip(slabs, h_s5):
        sre[:, sl] = hr
        sim[:, sl] = hi

    gates, out_ab, zc, out_c = {}, {}, {}, {}

    gates_full = [_sigmoid(proj(OFF_GL + j * D_MODEL, OFF_GL + (j + 1) * D_MODEL)) for j in range(3)]

    def stage_gates(p):
        t0, t1 = parts[p]
        gates[p] = [g[t0 * R:t1 * R, :] for g in gates_full]

    def stage_branches(p):
        t0, t1 = parts[p]
        rows = slice(t0 * R, t1 * R)
        out_ab[p] = (_dot(lhs_a[p], pa[...]), _dot(lhs_b[p], pb[...]))
        y = []
        for k in range(NS // C_BLK):
            hre = bu[rows, k * C_BLK:(k + 1) * C_BLK].astype(BF16)
            him = bu[rows, NS + k * C_BLK:NS + (k + 1) * C_BLK].astype(BF16)
            sl = slice(k * B_BLK, (k + 1) * B_BLK)
            y.append(_dot(hre, cre[k]) + _dot(him, cim[k]) + vrow(ROW_SSD, sl) * us_buf[rows, sl])
        zc[p] = jax.nn.gelu(jnp.concatenate(y, axis=1))

    def stage_glu(p):
        zc[p] = (zc[p] * _sigmoid(_dot(zc[p].astype(BF16), gluw[...]) + vrow(ROW_GLB))).astype(BF16)

    def stage_proj_c(p):
        out_c[p] = _dot(zc[p], pc[...])

    def stage_out(p):
        t0, t1 = parts[p]
        m = gates[p][0] * out_ab[p][0] + gates[p][1] * out_ab[p][1] + gates[p][2] * out_c[p]
        out = _dot(m.astype(BF16), wout[...])
        y1 = _layer_norm(ALPHA * _load_rows(x_ref, t0, t1) + out, vln[0:1, :], vln[1:2, :])
        _store_rows(x1_ref, y1, t0)

    for stage in (stage_gates, stage_branches, stage_glu, stage_proj_c, stage_out):
        for p in range(TIME_PARTS):
            stage(p)

    xa_ext[0:KA, :] = xa_ext[TR:TR + KA, :]
    u_ext[0:KB, :] = u_ext[TR:TR + KB, :]

    @pl.when(i == pl.num_programs(0) - 1)
    def _():
        for k in range(LRU_CONV - 1):
            ca_out[:, k * WIDTH:(k + 1) * WIDTH] = xa_ext[k * R:(k + 1) * R, :]
        for k in range(SC_CONV - 1):
            cb_out[:, k * WIDTH:(k + 1) * WIDTH] = u_ext[k * R:(k + 1) * R, :]
        ha_out[...] = ha[...]
        sre_out[...] = sre[...]
        sim_out[...] = sim[...]


def _layer_block(layer, shape):
    zeros = (0,) * (len(shape) - 1)
    return pl.BlockSpec((None,) + tuple(shape[1:]), lambda i: (layer,) + zeros, pipeline_mode=pl.Buffered(1))


def _mixer(x, layer, states, wts):
    R = x.shape[0]
    tile = MIXER_ROW_TILE
    T = tile // R
    n_steps = x.shape[1] // (T * D_MODEL)
    assert tile % R == 0 and x.shape[1] % (T * D_MODEL) == 0
    x_spec = pl.BlockSpec((R, T * D_MODEL), lambda i: (0, i))
    state_shapes = [(R, (LRU_CONV - 1) * WIDTH), (R, WIDTH), (R, (SC_CONV - 1) * WIDTH), (R, NS), (R, NS)]
    states = () if states is None else tuple(states)
    in_specs = ([x_spec] + [_layer_block(layer, s.shape) for s in states]
                + [_layer_block(layer, w.shape) for w in wts])
    out_shape = [jax.ShapeDtypeStruct(x.shape, F32)] + [jax.ShapeDtypeStruct(s, F32) for s in state_shapes]
    out_specs = [x_spec] + [pl.BlockSpec(s, lambda i: (0, 0)) for s in state_shapes]
    scratch = [
        pltpu.VMEM(((LRU_CONV - 1) * R + tile, WIDTH), F32),
        pltpu.VMEM(((SC_CONV - 1) * R + tile, WIDTH), F32),
        pltpu.VMEM((tile, WIDTH), F32),
        pltpu.VMEM((tile, 2 * NS), F32),
        pltpu.VMEM((R, WIDTH), F32),
        pltpu.VMEM((R, NS), F32),
        pltpu.VMEM((R, NS), F32),
    ]
    outs = pl.pallas_call(
        functools.partial(_mixer_kernel, R, bool(states)),
        grid=(n_steps,), in_specs=in_specs, out_specs=out_specs, out_shape=out_shape,
        scratch_shapes=scratch,
        compiler_params=pltpu.CompilerParams(dimension_semantics=("arbitrary",),
                                             vmem_limit_bytes=V7X_VMEM_LIMIT_BYTES),
        name=f"mixer_r{R}",
    )(x, *states, *wts)
    return outs[0], outs[1:]


def _mlp_block(x_ref, o_ref, up, down, vln):
    r = x_ref.shape[0]
    steps_all = x_ref.shape[1] // D_MODEL
    sub_steps = min(steps_all, MLP_SUB_ROWS // r)
    n_ff = D_FF // FF_BLK
    for s0 in range(0, steps_all, sub_steps):
        xb = _load_rows(x_ref, s0, s0 + sub_steps).astype(BF16)
        acc = None
        for c in range(n_ff):
            h = _dot(xb, up[:, c * FF_BLK:(c + 1) * FF_BLK].astype(BF16))
            h = jnp.square(jnp.maximum(h, 0.0)).astype(BF16)
            if c < n_ff - 1:
                d = _dot(h, down[c * FF_BLK:(c + 1) * FF_BLK, :].astype(BF16))
                acc = d if acc is None else acc + d
        down_last = down[(n_ff - 1) * FF_BLK:, :].astype(BF16)
        for part in range(TAIL_PARTS):
            t0, t1 = part * sub_steps // TAIL_PARTS, (part + 1) * sub_steps // TAIL_PARTS
            rows = slice(t0 * r, t1 * r)
            f = ALPHA * _load_rows(x_ref, s0 + t0, s0 + t1) + (acc[rows, :] + _dot(h[rows, :], down_last))
            _store_rows(o_ref, _layer_norm(f, vln[0:1, :], vln[1:2, :]), s0 + t0)


def _mlp_kernel(n_prompt, xp_ref, xs_ref, up, down, vln, op_ref, os_ref):
    i = pl.program_id(0)

    @pl.when(i < n_prompt)
    def _():
        _mlp_block(xp_ref, op_ref, up, down, vln)

    @pl.when(i == n_prompt)
    def _():
        _mlp_block(xs_ref, os_ref, up, down, vln)


def _mlp(xp, xs, layer, up, down, vln):
    rp, rs = xp.shape[0], xs.shape[0]
    tp = ROW_TILE // rp
    n_prompt = xp.shape[1] // (tp * D_MODEL)
    p_spec = pl.BlockSpec((rp, tp * D_MODEL), lambda i: (0, jnp.minimum(i, n_prompt - 1)))
    s_spec = pl.BlockSpec(xs.shape, lambda i: (0, 0))
    return pl.pallas_call(
        functools.partial(_mlp_kernel, n_prompt),
        grid=(n_prompt + 1,),
        in_specs=[p_spec, s_spec] + [_layer_block(layer, w.shape) for w in (up, down, vln)],
        out_specs=[p_spec, s_spec],
        out_shape=[jax.ShapeDtypeStruct(xp.shape, F32), jax.ShapeDtypeStruct(xs.shape, F32)],
        compiler_params=pltpu.CompilerParams(dimension_semantics=("arbitrary",),
                                             vmem_limit_bytes=V7X_VMEM_LIMIT_BYTES),
        name="mlp",
    )(xp, xs, up, down, vln)


def _block_diag(blocks):
    n, a, b = blocks.shape[-3:]
    eye = jnp.eye(n, dtype=blocks.dtype)
    out = jnp.einsum("...nab,nm->...namb", blocks, eye)
    return out.reshape(*blocks.shape[:-3], n * a, n * b)


def kernel(x_prompt, x_sample, state_lru_conv, state_lru_h, state_sconv, state_ssm_re, state_ssm_im,
           w_in, conv_a_w, conv_a_b, gate_x_w, gate_x_b, gate_a_w, gate_a_b, lru_lambda, conv_b_w,
           ssm_a_re, ssm_a_im, ssm_log_dt, ssm_b_re, ssm_b_im, ssm_c_re, ssm_c_im, ssm_d, glu_w, glu_b,
           proj_a, proj_b, proj_c, w_out, ln1_g, ln1_b, mlp_up, mlp_down, ln2_g, ln2_b):
    batch_p, batch_s = x_prompt.shape[0], x_sample.shape[0]
    bf = lambda w: w.astype(BF16)

    vab, bbar_re, bbar_im = _s5_prep(ssm_a_re, ssm_a_im, ssm_log_dt, ssm_b_re, ssm_b_im)

    def b_blocks(bbar):
        bb = bbar.reshape(DEPTH, SSM_GROUP, WIDTH // B_BLK, B_BLK // SSM_GROUP, SSM_STATE)
        return _block_diag(jnp.transpose(bb, (0, 2, 3, 1, 4)))

    bblk = bf(jnp.concatenate([b_blocks(bbar_re), b_blocks(bbar_im)], axis=-1))

    def c_blocks(c):
        cc = c.reshape(DEPTH, NS // C_BLK, C_BLK // SSM_STATE, SSM_GROUP, SSM_STATE)
        return bf(_block_diag(jnp.swapaxes(cc, -1, -2)))

    cre, cim = c_blocks(ssm_c_re), c_blocks(-ssm_c_im)

    def gate_blocks(w):
        heads = GATE_BLK // LRU_HEAD_DIM
        return bf(_block_diag(w.reshape(DEPTH, LRU_HEADS // heads, heads, LRU_HEAD_DIM, LRU_HEAD_DIM)))

    gxw, gaw = gate_blocks(gate_x_w), gate_blocks(gate_a_w)

    row = lambda v: v[:, None, :]
    vw = jnp.concatenate([conv_a_w, row(conv_a_b), row(gate_x_b), row(gate_a_b), row(lru_lambda),
                          conv_b_w, row(ssm_d), row(glu_b), jnp.zeros((DEPTH, 3, WIDTH), F32)], axis=1)
    vln1 = jnp.stack([ln1_g, ln1_b], axis=1)
    vln2 = jnp.stack([ln2_g, ln2_b], axis=1)
    wts = (bf(w_in), gxw, gaw, bf(proj_a), bf(proj_b), bf(proj_c), bf(glu_w), bf(w_out), bblk, cre, cim,
           vw, vln1, vab)
    up, down = mlp_up, mlp_down

    states_s = (state_lru_conv.reshape(DEPTH, batch_s, (LRU_CONV - 1) * WIDTH), state_lru_h,
                state_sconv.reshape(DEPTH, batch_s, (SC_CONV - 1) * WIDTH),
                state_ssm_re.reshape(DEPTH, batch_s, NS), state_ssm_im.reshape(DEPTH, batch_s, NS))

    yp = x_prompt.reshape(batch_p, -1)
    ys = x_sample.reshape(batch_s, -1)
    new_p, new_s = [], []
    for layer in range(DEPTH):
        ys, sts = _mixer(ys, layer, states_s, wts)
        yp, stp = _mixer(yp, layer, None, wts)
        yp, ys = _mlp(yp, ys, layer, up, down, vln2)
        new_p.append(stp)
        new_s.append(sts)

    def collect(new, batch):
        ca, ha, cb, sre, sim = (jnp.stack([st[j] for st in new], 0) for j in range(5))
        shape_ssm = (DEPTH, batch, SSM_GROUPS, SSM_STATE)
        return (ca.reshape(DEPTH, batch, LRU_CONV - 1, WIDTH), ha, cb.reshape(DEPTH, batch, SC_CONV - 1, WIDTH),
                sre.reshape(shape_ssm), sim.reshape(shape_ssm))

    return ((yp.reshape(x_prompt.shape), ys.reshape(x_sample.shape))
            + collect(new_p, batch_p) + collect(new_s, batch_s))
```
